```python
import math
import jax, jax.numpy as jnp
from jax import lax
import numpy as np

D_MODEL = 2048
BATCH = 1
SEQ = 8192
DEPTH = 2

D_FOURIER = D_MODEL // 4
FOURIER_GROUP = 128
N_FOURIER_GROUPS = D_FOURIER // FOURIER_GROUP
D_DIFF = D_MODEL // 2
DIFF_HEAD_DIM = 128
DIFF_HALF_DIM = DIFF_HEAD_DIM // 2
N_DIFF_HEADS = D_DIFF // DIFF_HEAD_DIM
D_GMLP = D_MODEL - D_FOURIER - D_DIFF
GMLP_GROUP = 128
N_GMLP_GROUPS = D_GMLP // GMLP_GROUP
CHUNK = 128
SPLIT_SIZES = (D_FOURIER, D_FOURIER, D_DIFF, D_DIFF, D_DIFF, D_DIFF, D_GMLP, D_GMLP, D_GMLP)
D_IN = sum(SPLIT_SIZES)
N_BUCKETS = 32
MAX_DISTANCE = 128
Q_BLOCK = 128
EPS = 1e-6

kernel_name = "hybrid_fourier_diffattn_sgu_encoder"


def rms_norm(x, g):
    xf = x.astype(jnp.float32)
    y = xf * lax.rsqrt(jnp.mean(xf * xf, axis=-1, keepdims=True) + EPS)
    return (y * g.astype(jnp.float32)).astype(x.dtype)


def t5_bucket(rel):
    nb = N_BUCKETS // 2
    ret = (rel > 0).astype(jnp.int32) * nb
    n = jnp.abs(rel)
    max_exact = nb // 2
    nf = jnp.maximum(n, 1).astype(jnp.float32)
    large = max_exact + (jnp.log(nf / max_exact) / math.log(MAX_DISTANCE / max_exact)
                         * (nb - max_exact)).astype(jnp.int32)
    large = jnp.minimum(large, nb - 1)
    return ret + jnp.where(n < max_exact, n, large)


def fourier_mix(xa, w_f):
    b, s, _ = xa.shape
    xg = xa.astype(jnp.float32).reshape(b, s, N_FOURIER_GROUPS, FOURIER_GROUP)
    f = jnp.fft.fft2(xg, axes=(1, 3), norm="ortho").real
    f = f.reshape(b, s, D_FOURIER).astype(xa.dtype)
    return f @ w_f


def diff_attention(q, k, v, rel_bias, lam, out_gain, lambda_init):
    b, s, _ = q.shape
    scale = DIFF_HALF_DIM ** -0.5
    q = q.reshape(b, s, N_DIFF_HEADS, 2, DIFF_HALF_DIM) * scale
    k = k.reshape(b, s, N_DIFF_HEADS, 2, DIFF_HALF_DIM)
    v = v.reshape(b, s, N_DIFF_HEADS, DIFF_HEAD_DIM)
    n_blocks = s // Q_BLOCK
    qb = q.reshape(b, n_blocks, Q_BLOCK, N_DIFF_HEADS, 2, DIFF_HALF_DIM).transpose(1, 0, 2, 3, 4, 5)
    starts = jnp.arange(n_blocks, dtype=jnp.int32) * Q_BLOCK
    k_pos = jnp.arange(s, dtype=jnp.int32)

    def block(args):
        q_blk, start = args
        logits = jnp.einsum('bqhcd,bkhcd->bhcqk', q_blk, k,
                            preferred_element_type=jnp.float32)
        q_pos = start + jnp.arange(Q_BLOCK, dtype=jnp.int32)
        bucket = t5_bucket(k_pos[None, :] - q_pos[:, None])
        bias = jnp.take(rel_bias, bucket, axis=0).astype(jnp.float32).transpose(2, 0, 1)
        p = jax.nn.softmax(logits + bias[None, :, None], axis=-1)
        w = p[:, :, 0] - lam * p[:, :, 1]
        return jnp.einsum('bhqk,bkhd->bqhd', w.astype(v.dtype), v)

    o = lax.map(block, (qb, starts))
    o = o.transpose(1, 0, 2, 3, 4).reshape(b, s, N_DIFF_HEADS, DIFF_HEAD_DIM)
    o = rms_norm(o, out_gain.reshape(N_DIFF_HEADS, DIFF_HEAD_DIM)) * (1.0 - lambda_init)
    return o.reshape(b, s, D_DIFF)


def spatial_gating(u, v, v_gain, w_s, b_s):
    b, s, _ = u.shape
    vg = rms_norm(v.reshape(b, s, N_GMLP_GROUPS, GMLP_GROUP),
                  v_gain.reshape(N_GMLP_GROUPS, GMLP_GROUP))
    vg = vg.reshape(b, s // CHUNK, CHUNK, N_GMLP_GROUPS, GMLP_GROUP)
    mixed = jnp.einsum('gpq,bnqgc->bnpgc', w_s, vg) + b_s.T[None, None, :, :, None]
    return u * mixed.reshape(b, s, D_GMLP)


def setup_inputs(seed: int = 0) -> dict:
    key = jax.random.key(seed)
    ks = jax.random.split(key, 13)
    nrm = jax.random.normal
    x = nrm(ks[0], (BATCH, SEQ, D_MODEL), jnp.float32)
    w_in = nrm(ks[1], (DEPTH, D_MODEL, D_IN), jnp.float32) * D_MODEL ** -0.5
    pre_gain = 1.0 + 0.02 * nrm(ks[2], (DEPTH, D_MODEL), jnp.float32)
    post_gain = 1.0 + 0.02 * nrm(ks[3], (DEPTH, D_MODEL), jnp.float32)
    w_fourier = nrm(ks[4], (DEPTH, D_FOURIER, D_FOURIER), jnp.float32) * D_FOURIER ** -0.5
    lambda_qk = 0.1 * nrm(ks[5], (DEPTH, 4, DIFF_HALF_DIM), jnp.float32)
    diff_out_gain = 1.0 + 0.02 * nrm(ks[6], (DEPTH, D_DIFF), jnp.float32)
    sg_v_gain = 1.0 + 0.02 * nrm(ks[7], (DEPTH, D_GMLP), jnp.float32)
    w_spatial = nrm(ks[8], (DEPTH, N_GMLP_GROUPS, CHUNK, CHUNK), jnp.float32) * CHUNK ** -0.5
    b_spatial = 1.0 + 0.02 * nrm(ks[9], (DEPTH, N_GMLP_GROUPS, CHUNK), jnp.float32)
    w_out = nrm(ks[10], (DEPTH, D_MODEL, D_MODEL), jnp.float32) * D_MODEL ** -0.5
    rel_bias = 0.5 * nrm(ks[11], (N_BUCKETS, N_DIFF_HEADS), jnp.float32)
    return {"x": x, "w_in": w_in, "pre_gain": pre_gain, "post_gain": post_gain,
            "w_fourier": w_fourier, "lambda_qk": lambda_qk, "diff_out_gain": diff_out_gain,
            "sg_v_gain": sg_v_gain, "w_spatial": w_spatial, "b_spatial": b_spatial,
            "w_out": w_out, "rel_bias": rel_bias}


def reference(x, w_in, pre_gain, post_gain, w_fourier, lambda_qk, diff_out_gain,
              sg_v_gain, w_spatial, b_spatial, w_out, rel_bias):
    split_points = [int(o) for o in np.cumsum(SPLIT_SIZES)[:-1]]
    for l in range(DEPTH):
        lambda_init = 0.8 - 0.6 * math.exp(-0.3 * l)
        h = rms_norm(x, pre_gain[l])
        proj = h @ w_in[l]
        a_in, a_gate, q, k, v, b_gate, u, v_sg, c_gate = jnp.split(proj, split_points, axis=-1)
        y_a = fourier_mix(a_in, w_fourier[l])
        lq = lambda_qk[l].astype(jnp.float32)
        lam = jnp.exp(jnp.sum(lq[0] * lq[1])) - jnp.exp(jnp.sum(lq[2] * lq[3])) + lambda_init
        y_b = diff_attention(q, k, v, rel_bias, lam, diff_out_gain[l], lambda_init)
        y_c = spatial_gating(u, v_sg, sg_v_gain[l], w_spatial[l], b_spatial[l])
        y = jnp.concatenate([y_a * jax.nn.silu(a_gate),
                             y_b * jax.nn.silu(b_gate),
                             y_c * jax.nn.silu(c_gate)], axis=-1) @ w_out[l]
        x = x + rms_norm(y, post_gain[l])
    return x
```

```python
import functools
import math

import numpy as np
import jax
import jax.numpy as jnp
from jax import lax
from jax.experimental import pallas as pl
from jax.experimental.pallas import tpu as pltpu

F32 = jnp.float32
BF16 = jnp.bfloat16

EPS = 1e-6
LOG2E = math.log2(math.e)
NEG_INIT = -1e30

D_FOURIER = 512
D_DIFF = 1024
D_GMLP = 512
GROUP = 128
HEAD_DIM = 128
HALF_DIM = 64
N_HEADS = 8
CHUNK = 128
N_BUCKETS = 32
MAX_DISTANCE = 128
OFF_A_IN, OFF_A_GATE, OFF_Q, OFF_K, OFF_V, OFF_B_GATE, OFF_U, OFF_VSG, OFF_C_GATE = (
    0, 512, 1024, 2048, 3072, 4096, 5120, 5632, 6144)
D_IN = 6656

TM_IN = 512
TN_IN = 1664
TM_OUT = 512
T_ATT = 512
FFT_N2 = 128
FFT_TN = 8192
VMEM_LIMIT = 56 * 1024 * 1024


def _silu(x):
    return x * (1.0 / (1.0 + jnp.exp(-x)))


def _inproj_kernel(x_ref, g_ref, w_ref, cs_ref, proj_ref, ain_ref, a_scr):
    j = pl.program_id(1)

    @pl.when(j == 0)
    def _():
        x = x_ref[...]
        ms = jnp.mean(x * x, axis=-1, keepdims=True)
        a_scr[...] = ((x * lax.rsqrt(ms + EPS)) * g_ref[...]).astype(BF16)

    acc = jnp.dot(a_scr[...], w_ref[...], preferred_element_type=F32)
    proj_ref[...] = (acc * cs_ref[...]).astype(BF16)

    @pl.when(j == 0)
    def _():
        ain_ref[...] = acc[:, :D_FOURIER]


def _inproj(x2, gain, w_bf, colscale):
    s, d = x2.shape
    n = w_bf.shape[1]
    grid = (s // TM_IN, n // TN_IN)
    return pl.pallas_call(
        _inproj_kernel,
        grid=grid,
        in_specs=[
            pl.BlockSpec((TM_IN, d), lambda i, j: (i, 0)),
            pl.BlockSpec((1, d), lambda i, j: (0, 0)),
            pl.BlockSpec((d, TN_IN), lambda i, j: (0, j)),
            pl.BlockSpec((1, TN_IN), lambda i, j: (0, j)),
        ],
        out_specs=[
            pl.BlockSpec((TM_IN, TN_IN), lambda i, j: (i, j)),
            pl.BlockSpec((TM_IN, D_FOURIER), lambda i, j: (i, 0)),
        ],
        out_shape=[
            jax.ShapeDtypeStruct((s, n), BF16),
            jax.ShapeDtypeStruct((s, D_FOURIER), F32),
        ],
        scratch_shapes=[pltpu.VMEM((TM_IN, d), BF16)],
        compiler_params=pltpu.CompilerParams(
            dimension_semantics=("arbitrary", "arbitrary"),
            vmem_limit_bytes=VMEM_LIMIT),
        name="inproj",
    )(x2, gain, w_bf, colscale)


def _dft_consts(n1, n2, c):
    def cs(n):
        idx = np.arange(n)
        ang = 2.0 * np.pi * ((idx[:, None] * idx[None, :]) % n) / n
        return np.cos(ang), np.sin(ang)

    c1, s1 = cs(n1)
    c2, s2 = cs(n2)
    cc, sc = cs(c)
    f1 = np.concatenate([c1, -s1], axis=0)
    f2 = np.block([[c2, s2], [s2, -c2]])
    n = n1 * n2
    k1 = np.arange(n1)[:, None]
    m2 = np.arange(n2)[None, :]
    ang = 2.0 * np.pi * ((k1 * m2) % n) / n
    tw_c = np.cos(ang)[:, :, None]
    tw_s = np.sin(ang)[:, :, None]
    groups = D_FOURIER // c
    bdc = np.kron(np.eye(groups), cc)
    bds = -np.kron(np.eye(groups), sc)
    f32 = lambda a: jnp.asarray(a.astype(np.float32))
    return f32(f1), f32(f2), f32(tw_c), f32(tw_s), f32(bdc), f32(bds)


def _fft1_kernel(f1_ref, x_ref, y_ref):
    y_ref[...] = jnp.dot(f1_ref[...].astype(BF16), x_ref[...].astype(BF16),
                         preferred_element_type=F32)


def _fft1(f1, xv):
    n1, cols = xv.shape
    tn = min(FFT_TN, cols)
    return pl.pallas_call(
        _fft1_kernel,
        grid=(cols // tn,),
        in_specs=[pl.BlockSpec((2 * n1, n1), lambda j: (0, 0)),
                  pl.BlockSpec((n1, tn), lambda j: (0, j))],
        out_specs=pl.BlockSpec((2 * n1, tn), lambda j: (0, j)),
        out_shape=jax.ShapeDtypeStruct((2 * n1, cols), F32),
        compiler_params=pltpu.CompilerParams(
            dimension_semantics=("arbitrary",), vmem_limit_bytes=VMEM_LIMIT),
        name="fft_stage1",
    )(f1, xv)


def _fft2_kernel(y_ref, twc_ref, tws_ref, f2_ref, bdc_ref, bds_ref, wf_ref, gate_ref,
                 o_ref, *, scale):
    yr = y_ref[0, 0]
    yi = y_ref[1, 0]
    tc = twc_ref[0]
    ts = tws_ref[0]
    zr = yr * tc + yi * ts
    zi = yi * tc - yr * ts
    z = jnp.concatenate([zr, zi], axis=0).astype(BF16)
    pq = jnp.dot(f2_ref[...].astype(BF16), z, preferred_element_type=F32)
    n2 = yr.shape[0]
    p = pq[:n2].astype(BF16)
    q = pq[n2:].astype(BF16)
    f = (jnp.dot(p, bdc_ref[...].astype(BF16), preferred_element_type=F32)
         + jnp.dot(q, bds_ref[...].astype(BF16), preferred_element_type=F32)) * scale
    ya = jnp.dot(f.astype(BF16), wf_ref[...], preferred_element_type=F32)
    o_ref[...] = (ya * _silu(gate_ref[...].astype(F32))).astype(BF16)


def _fft2(y4, twc, tws, f2, bdc, bds, wf_bf, proj_v, scale):
    _, n1, n2, c = y4.shape
    gate_blk = OFF_A_GATE // c
    blk_per_row = D_IN // c
    const = lambda shape: pl.BlockSpec(shape, lambda k: tuple(0 for _ in shape))
    return pl.pallas_call(
        functools.partial(_fft2_kernel, scale=scale),
        grid=(n1,),
        in_specs=[
            pl.BlockSpec((2, 1, n2, c), lambda k: (0, k, 0, 0)),
            pl.BlockSpec((1, n2, 1), lambda k: (k, 0, 0)),
            pl.BlockSpec((1, n2, 1), lambda k: (k, 0, 0)),
            const((2 * n2, 2 * n2)),
            const((c, c)),
            const((c, c)),
            const((c, c)),
            pl.BlockSpec((n2, c), lambda k: (0, k * blk_per_row + gate_blk)),
        ],
        out_specs=pl.BlockSpec((n2, c), lambda k: (0, k)),
        out_shape=jax.ShapeDtypeStruct((n2, n1 * c), BF16),
        compiler_params=pltpu.CompilerParams(
            dimension_semantics=("arbitrary",), vmem_limit_bytes=VMEM_LIMIT),
        name="fft_stage2",
    )(y4, twc, tws, f2, bdc, bds, wf_bf, proj_v)


def _fourier_branch(a_in, proj, wf_bf, consts):
    s, c = a_in.shape
    n2 = FFT_N2
    n1 = s // n2
    f1, f2, twc, tws, bdc, bds = consts
    y = _fft1(f1, a_in.reshape(n1, n2 * c))
    y4 = y.reshape(2, n1, n2, c)
    scale = 1.0 / math.sqrt(s * GROUP)
    out = _fft2(y4, twc, tws, f2, bdc, bds, wf_bf, proj.reshape(n2, n1 * D_IN), scale)
    return out.reshape(s, c)


def _t5_bucket(rel):
    nb = N_BUCKETS // 2
    ret = (rel > 0).astype(jnp.int32) * nb
    n = jnp.abs(rel)
    max_exact = nb // 2
    nf = jnp.maximum(n, 1).astype(jnp.float32)
    large = max_exact + (jnp.log(nf / max_exact) / math.log(MAX_DISTANCE / max_exact)
                         * (nb - max_exact)).astype(jnp.int32)
    large = jnp.minimum(large, nb - 1)
    return ret + jnp.where(n < max_exact, n, large)


def _bias_kernel(rb_ref, bk_ref, o_ref):
    h = pl.program_id(0)
    bk = bk_ref[0]
    acc = jnp.zeros(bk.shape, F32)
    for b in range(N_BUCKETS):
        acc = jnp.where(bk == b, rb_ref[b, h], acc)
    o_ref[0, 0] = acc * LOG2E


def _bias_tiles(rel_bias, t):
    kk = jnp.arange(t, dtype=jnp.int32)[:, None]
    rr = jnp.arange(t, dtype=jnp.int32)[None, :]
    rel = jnp.stack([d * t + kk - rr for d in (-1, 0, 1)])
    bucket = _t5_bucket(rel)
    return pl.pallas_call(
        _bias_kernel,
        grid=(N_HEADS, 3),
        in_specs=[pl.BlockSpec(memory_space=pltpu.SMEM),
                  pl.BlockSpec((1, t, t), lambda h, d: (d, 0, 0))],
        out_specs=pl.BlockSpec((1, 1, t, t), lambda h, d: (h, d, 0, 0)),
        out_shape=jax.ShapeDtypeStruct((N_HEADS, 3, t, t), F32),
        compiler_params=pltpu.CompilerParams(
            dimension_semantics=("arbitrary", "arbitrary"), vmem_limit_bytes=VMEM_LIMIT),
        name="bias_tiles",
    )(rel_bias, bucket)


def _attn_kernel(lq_ref, qt_ref, k_ref, vt_ref, bias_ref, gate_ref, gain_ref, o_ref,
                 q2_scr, m_scr, l_scr, acc_scr, *, lambda_init, n_tiles):
    t = qt_ref.shape[-1]
    i = pl.program_id(1)

    qt = qt_ref[0, 0]
    row = lax.broadcasted_iota(jnp.int32, qt.shape, 0)
    zero = jnp.zeros_like(qt)
    q2_scr[:, :t] = jnp.where(row < HALF_DIM, qt, zero)
    q2_scr[:, t:] = jnp.where(row >= HALF_DIM, qt, zero)
    m_scr[...] = jnp.full(m_scr.shape, NEG_INIT, F32)
    l_scr[...] = jnp.zeros(l_scr.shape, F32)
    acc_scr[...] = jnp.zeros(acc_scr.shape, F32)

    def step(j, bias):
        kt = k_ref[pl.ds(pl.multiple_of(j * t, t), t), :]
        s = jnp.dot(kt, q2_scr[...], preferred_element_type=F32)
        if bias is not None:
            s = s + jnp.concatenate([bias, bias], axis=1)
        m_old = m_scr[...]
        m_new = jnp.maximum(m_old, jnp.max(s, axis=0, keepdims=True))
        p = jnp.exp2(s - m_new)
        alpha = jnp.exp2(m_old - m_new)
        l_scr[...] = alpha * l_scr[...] + jnp.sum(p, axis=0, keepdims=True)
        pv = jnp.dot(vt_ref[0, j], p.astype(BF16), preferred_element_type=F32)
        acc_scr[...] = alpha * acc_scr[...] + pv
        m_scr[...] = m_new

    def far_body(j, carry):
        step(j, None)
        return carry

    c_left = bias_ref[0, 0, 0:1, t - 1:t]
    c_right = bias_ref[0, 2, t - 1:t, 0:1]

    lax.fori_loop(0, i - 1, far_body, 0)
    m_scr[...] = m_scr[...] + c_left
    for d in (-1, 0, 1):
        j = i + d

        @pl.when(jnp.logical_and(j >= 0, j < n_tiles))
        def _(j=j, d=d):
            step(j, bias_ref[0, d + 1])

    m_scr[...] = m_scr[...] - c_right
    lax.fori_loop(i + 2, n_tiles, far_body, 0)

    lq = lq_ref[...]
    lam = (jnp.exp(jnp.sum(lq[0:1] * lq[1:2], axis=-1, keepdims=True))
           - jnp.exp(jnp.sum(lq[2:3] * lq[3:4], axis=-1, keepdims=True)) + lambda_init)
    accn = acc_scr[...] * (1.0 / l_scr[...])
    o_t = accn[:, :t] - lam * accn[:, t:]
    o = o_t.T
    ms = jnp.mean(o * o, axis=-1, keepdims=True)
    y = ((o * lax.rsqrt(ms + EPS)) * gain_ref[...]) * (1.0 - lambda_init)
    o_ref[...] = (y * _silu(gate_ref[...].astype(F32))).astype(BF16)


def _attention(lq, qt4, proj, vt4, bias, gain, lambda_init):
    s = proj.shape[0]
    t = qt4.shape[-1]
    n_tiles = s // t
    kblk = OFF_K // HEAD_DIM
    gblk = OFF_B_GATE // HEAD_DIM
    kern = functools.partial(_attn_kernel, lambda_init=lambda_init, n_tiles=n_tiles)
    return pl.pallas_call(
        kern,
        grid=(N_HEADS, n_tiles),
        in_specs=[
            pl.BlockSpec((4, HALF_DIM), lambda h, i: (0, 0)),
            pl.BlockSpec((1, 1, HEAD_DIM, t), lambda h, i: (h, i, 0, 0)),
            pl.BlockSpec((s, HEAD_DIM), lambda h, i: (0, kblk + h)),
            pl.BlockSpec((1, n_tiles, HEAD_DIM, t), lambda h, i: (h, 0, 0, 0)),
            pl.BlockSpec((1, 3, t, t), lambda h, i: (h, 0, 0, 0)),
            pl.BlockSpec((t, HEAD_DIM), lambda h, i: (i, gblk + h)),
            pl.BlockSpec((1, HEAD_DIM), lambda h, i: (0, h)),
        ],
        out_specs=pl.BlockSpec((t, HEAD_DIM), lambda h, i: (i, h)),
        out_shape=jax.ShapeDtypeStruct((s, D_DIFF), BF16),
        scratch_shapes=[
            pltpu.VMEM((HEAD_DIM, 2 * t), BF16),
            pltpu.VMEM((1, 2 * t), F32),
            pltpu.VMEM((1, 2 * t), F32),
            pltpu.VMEM((HEAD_DIM, 2 * t), F32),
        ],
        compiler_params=pltpu.CompilerParams(
            dimension_semantics=("arbitrary", "arbitrary"), vmem_limit_bytes=VMEM_LIMIT),
        name="diff_attention",
    )(lq, qt4, proj, vt4, bias, proj, gain)


def _head_transposed(proj, off, t):
    s = proj.shape[0]
    a = proj[:, off:off + D_DIFF].reshape(s // t, t, N_HEADS, HEAD_DIM)
    return a.transpose(2, 0, 3, 1)


def _outproj_kernel(ya_ref, yb_ref, u_ref, vsg_ref, cg_ref, x_ref, wout_ref, ws_ref, bs_ref,
                    vgain_ref, pgain_ref, o_ref, y_scr):
    tm = x_ref.shape[0]
    y_scr[:, 0:D_FOURIER] = ya_ref[...]
    y_scr[:, D_FOURIER:D_FOURIER + D_DIFF] = yb_ref[...]
    for g in range(D_GMLP // GROUP):
        cols = slice(g * GROUP, (g + 1) * GROUP)
        v = vsg_ref[:, cols].astype(F32)
        ms = jnp.mean(v * v, axis=-1, keepdims=True)
        vn = ((v * lax.rsqrt(ms + EPS)) * vgain_ref[:, cols]).astype(BF16)
        w = ws_ref[g]
        b = bs_ref[g]
        for c in range(tm // CHUNK):
            rows = slice(c * CHUNK, (c + 1) * CHUNK)
            mixed = jnp.dot(w, vn[rows], preferred_element_type=F32) + b
            gate = cg_ref[rows, cols].astype(F32)
            yc = (u_ref[rows, cols].astype(F32) * mixed) * _silu(gate)
            y_scr[rows, D_FOURIER + D_DIFF + g * GROUP:D_FOURIER + D_DIFF + (g + 1) * GROUP] = (
                yc.astype(BF16))
    y = jnp.dot(y_scr[...], wout_ref[...], preferred_element_type=F32)
    ms = jnp.mean(y * y, axis=-1, keepdims=True)
    o_ref[...] = x_ref[...] + (y * lax.rsqrt(ms + EPS)) * pgain_ref[...]


def _outproj(ya, yb, proj, x2, wout_bf, ws_bf, bs3, vgain, pgain):
    s, d = x2.shape
    tm = TM_OUT
    ublk, vblk, cblk = OFF_U // D_GMLP, OFF_VSG // D_GMLP, OFF_C_GATE // D_GMLP
    const = lambda shape: pl.BlockSpec(shape, lambda i: tuple(0 for _ in shape))
    return pl.pallas_call(
        _outproj_kernel,
        grid=(s // tm,),
        in_specs=[
            pl.BlockSpec((tm, D_FOURIER), lambda i: (i, 0)),
            pl.BlockSpec((tm, D_DIFF), lambda i: (i, 0)),
            pl.BlockSpec((tm, D_GMLP), lambda i: (i, ublk)),
            pl.BlockSpec((tm, D_GMLP), lambda i: (i, vblk)),
            pl.BlockSpec((tm, D_GMLP), lambda i: (i, cblk)),
            pl.BlockSpec((tm, d), lambda i: (i, 0)),
            const((d, d)),
            const(ws_bf.shape),
            const(bs3.shape),
            const((1, D_GMLP)),
            const((1, d)),
        ],
        out_specs=pl.BlockSpec((tm, d), lambda i: (i, 0)),
        out_shape=jax.ShapeDtypeStruct((s, d), F32),
        scratch_shapes=[pltpu.VMEM((tm, d), BF16)],
        compiler_params=pltpu.CompilerParams(
            dimension_semantics=("arbitrary",), vmem_limit_bytes=VMEM_LIMIT),
        name="outproj",
    )(ya, yb, proj, proj, proj, x2, wout_bf, ws_bf, bs3, vgain, pgain)


def kernel(x, w_in, pre_gain, post_gain, w_fourier, lambda_qk, diff_out_gain, sg_v_gain,
           w_spatial, b_spatial, w_out, rel_bias):
    b, s, d = x.shape
    depth = w_in.shape[0]
    assert b == 1 and w_in.shape[2] == D_IN
    x2 = x.reshape(s, d)

    q_scale = (HALF_DIM ** -0.5) * LOG2E
    colscale = jnp.ones((1, D_IN), F32).at[:, OFF_Q:OFF_Q + D_DIFF].set(q_scale)
    bias = _bias_tiles(rel_bias, T_ATT)
    consts = _dft_consts(s // FFT_N2, FFT_N2, GROUP)

    for l in range(depth):
        lambda_init = 0.8 - 0.6 * math.exp(-0.3 * l)
        proj, a_in = _inproj(x2, pre_gain[l].reshape(1, d), w_in[l].astype(BF16), colscale)
        ya = _fourier_branch(a_in, proj, w_fourier[l].astype(BF16), consts)
        qt4 = _head_transposed(proj, OFF_Q, T_ATT)
        vt4 = _head_transposed(proj, OFF_V, T_ATT)
        yb = _attention(lambda_qk[l], qt4, proj, vt4, bias,
                        diff_out_gain[l].reshape(1, D_DIFF), lambda_init)
        x2 = _outproj(ya, yb, proj, x2, w_out[l].astype(BF16), w_spatial[l].astype(BF16),
                      b_spatial[l].reshape(D_GMLP // GROUP, CHUNK, 1),
                      sg_v_gain[l].reshape(1, D_GMLP), post_gain[l].reshape(1, d))
    return x2.reshape(b, s, d)
```

```python
import functools
import math

import numpy as np
import jax
import jax.numpy as jnp
from jax import lax
from jax.experimental import pallas as pl
from jax.experimental.pallas import tpu as pltpu

F32 = jnp.float32
BF16 = jnp.bfloat16

EPS = 1e-6
LOG2E = math.log2(math.e)
NEG_INIT = -1e30
EXP2_SPAN = 120.0

D_FOURIER = 512
D_DIFF = 1024
D_GMLP = 512
GROUP = 128
HEAD_DIM = 128
HALF_DIM = 64
N_HEADS = 8
CHUNK = 128
N_BUCKETS = 32
MAX_DISTANCE = 128
OFF_A_IN, OFF_A_GATE, OFF_Q, OFF_K, OFF_V, OFF_B_GATE, OFF_U, OFF_VSG, OFF_C_GATE = (
    0, 512, 1024, 2048, 3072, 4096, 5120, 5632, 6144)
D_IN = 6656

MXU_COLS = 256
TM_IN = 512
TN_IN = 1664
TM_OUT = 512
T_ATT = 512
FFT_N2 = 128
FFT_ROWS = 8
VMEM_LIMIT = 56 * 1024 * 1024


def _silu(x):
    return x * (1.0 / (1.0 + jnp.exp(-x)))


def _inproj_kernel(x_ref, g_ref, w_ref, cs_ref, proj_ref, ain_ref, a_scr):
    j = pl.program_id(1)

    @pl.when(j == 0)
    def _():
        x = x_ref[...]
        ms = jnp.mean(x * x, axis=-1, keepdims=True)
        a_scr[...] = ((x * lax.rsqrt(ms + EPS)) * g_ref[...]).astype(BF16)

    acc = jnp.dot(a_scr[...], w_ref[...], preferred_element_type=F32)
    proj_ref[...] = (acc * cs_ref[...]).astype(BF16)

    @pl.when(j == 0)
    def _():
        ain_ref[...] = acc[:, :D_FOURIER]


def _inproj(x2, gain, w_bf, colscale):
    s, d = x2.shape
    n = w_bf.shape[1]
    grid = (s // TM_IN, n // TN_IN)
    return pl.pallas_call(
        _inproj_kernel,
        grid=grid,
        in_specs=[
            pl.BlockSpec((TM_IN, d), lambda i, j: (i, 0)),
            pl.BlockSpec((1, d), lambda i, j: (0, 0)),
            pl.BlockSpec((d, TN_IN), lambda i, j: (0, j)),
            pl.BlockSpec((1, TN_IN), lambda i, j: (0, j)),
        ],
        out_specs=[
            pl.BlockSpec((TM_IN, TN_IN), lambda i, j: (i, j)),
            pl.BlockSpec((TM_IN, D_FOURIER), lambda i, j: (i, 0)),
        ],
        out_shape=[
            jax.ShapeDtypeStruct((s, n), BF16),
            jax.ShapeDtypeStruct((s, D_FOURIER), F32),
        ],
        scratch_shapes=[pltpu.VMEM((TM_IN, d), BF16)],
        compiler_params=pltpu.CompilerParams(
            dimension_semantics=("arbitrary", "arbitrary"),
            vmem_limit_bytes=VMEM_LIMIT),
        name="inproj",
    )(x2, gain, w_bf, colscale)


def _dft_consts(n1, n2, c):
    def cs(n):
        idx = np.arange(n)
        ang = 2.0 * np.pi * ((idx[:, None] * idx[None, :]) % n) / n
        return np.cos(ang), np.sin(ang)

    c1, s1 = cs(n1)
    c2, s2 = cs(n2)
    cc, sc = cs(c)
    f1 = np.concatenate([c1, -s1], axis=0)
    f2 = np.block([[c2, s2], [s2, -c2]])
    n = n1 * n2
    k1 = np.arange(n1)[:, None]
    m2 = np.arange(n2)[None, :]
    ang = 2.0 * np.pi * ((k1 * m2) % n) / n
    tw_c = np.cos(ang)[:, :, None]
    tw_s = np.sin(ang)[:, :, None]
    groups = D_FOURIER // c
    bdc = np.kron(np.eye(groups), cc)
    bds = -np.kron(np.eye(groups), sc)
    f32 = lambda a: jnp.asarray(a.astype(np.float32))
    return f32(f1), f32(f2), f32(tw_c), f32(tw_s), f32(bdc), f32(bds)


def _fft1_kernel(f1_ref, x_ref, y_ref):
    f1 = f1_ref[...].astype(BF16)
    for r in range(x_ref.shape[1]):
        xr = x_ref[:, r, :].astype(BF16)
        y_ref[:, r, :] = jnp.dot(f1, xr, preferred_element_type=F32)


def _fft1(f1, x3):
    n1, n2, c = x3.shape
    return pl.pallas_call(
        _fft1_kernel,
        grid=(n2 // FFT_ROWS,),
        in_specs=[pl.BlockSpec((2 * n1, n1), lambda j: (0, 0)),
                  pl.BlockSpec((n1, FFT_ROWS, c), lambda j: (0, j, 0))],
        out_specs=pl.BlockSpec((2 * n1, FFT_ROWS, c), lambda j: (0, j, 0)),
        out_shape=jax.ShapeDtypeStruct((2 * n1, n2, c), F32),
        compiler_params=pltpu.CompilerParams(
            dimension_semantics=("arbitrary",), vmem_limit_bytes=VMEM_LIMIT),
        name="fft_stage1",
    )(f1, x3)


def _fft2_kernel(y_ref, twc_ref, tws_ref, f2_ref, bdc_ref, bds_ref, wf_ref, o_ref, *, scale):
    f2 = f2_ref[...].astype(BF16)
    bdc = bdc_ref[...].astype(BF16)
    bds = bds_ref[...].astype(BF16)
    n2 = y_ref.shape[2]
    for r in range(y_ref.shape[1]):
        yr = y_ref[0, r]
        yi = y_ref[1, r]
        tc = twc_ref[r]
        ts = tws_ref[r]
        zr = yr * tc + yi * ts
        zi = yi * tc - yr * ts
        z = jnp.concatenate([zr, zi], axis=0).astype(BF16)
        pq = jnp.dot(f2, z, preferred_element_type=F32)
        p = pq[:n2].astype(BF16)
        q = pq[n2:].astype(BF16)
        f = (jnp.dot(p, bdc, preferred_element_type=F32)
             + jnp.dot(q, bds, preferred_element_type=F32)) * scale
        o_ref[:, r, :] = jnp.dot(f.astype(BF16), wf_ref[...], preferred_element_type=F32)


def _fft2(y4, twc, tws, f2, bdc, bds, wf_bf, scale):
    _, n1, n2, c = y4.shape
    const = lambda shape: pl.BlockSpec(shape, lambda k: tuple(0 for _ in shape))
    return pl.pallas_call(
        functools.partial(_fft2_kernel, scale=scale),
        grid=(n1 // FFT_ROWS,),
        in_specs=[
            pl.BlockSpec((2, FFT_ROWS, n2, c), lambda k: (0, k, 0, 0)),
            pl.BlockSpec((FFT_ROWS, n2, 1), lambda k: (k, 0, 0)),
            pl.BlockSpec((FFT_ROWS, n2, 1), lambda k: (k, 0, 0)),
            const((2 * n2, 2 * n2)),
            const((c, c)),
            const((c, c)),
            const((c, c)),
        ],
        out_specs=pl.BlockSpec((n2, FFT_ROWS, c), lambda k: (0, k, 0)),
        out_shape=jax.ShapeDtypeStruct((n2, n1, c), F32),
        compiler_params=pltpu.CompilerParams(
            dimension_semantics=("arbitrary",), vmem_limit_bytes=VMEM_LIMIT),
        name="fft_stage2",
    )(y4, twc, tws, f2, bdc, bds, wf_bf)


def _fourier_branch(a_in, wf_bf, consts):
    s, c = a_in.shape
    n2 = FFT_N2
    n1 = s // n2
    f1, f2, twc, tws, bdc, bds = consts
    y = _fft1(f1, a_in.reshape(n1, n2, c))
    scale = 1.0 / math.sqrt(s * GROUP)
    out = _fft2(y.reshape(2, n1, n2, c), twc, tws, f2, bdc, bds, wf_bf, scale)
    return out.reshape(s, c)


def _t5_bucket(rel):
    nb = N_BUCKETS // 2
    ret = (rel > 0).astype(jnp.int32) * nb
    n = jnp.abs(rel)
    max_exact = nb // 2
    nf = jnp.maximum(n, 1).astype(jnp.float32)
    large = max_exact + (jnp.log(nf / max_exact) / math.log(MAX_DISTANCE / max_exact)
                         * (nb - max_exact)).astype(jnp.int32)
    large = jnp.minimum(large, nb - 1)
    return ret + jnp.where(n < max_exact, n, large)


def _bias_kernel(rb_ref, bk_ref, o_ref):
    h = pl.program_id(0)
    bk = bk_ref[0]
    acc = jnp.zeros(bk.shape, F32)
    for b in range(N_BUCKETS):
        acc = jnp.where(bk == b, rb_ref[b, h], acc)
    o_ref[0, 0] = acc * LOG2E


def _bias_tiles(rel_bias, t):
    kk = jnp.arange(t, dtype=jnp.int32)[:, None]
    rr = jnp.arange(t, dtype=jnp.int32)[None, :]
    rel = jnp.stack([d * t + kk - rr for d in (-1, 0, 1)])
    bucket = _t5_bucket(rel)
    return pl.pallas_call(
        _bias_kernel,
        grid=(N_HEADS, 3),
        in_specs=[pl.BlockSpec(memory_space=pltpu.SMEM),
                  pl.BlockSpec((1, t, t), lambda h, d: (d, 0, 0))],
        out_specs=pl.BlockSpec((1, 1, t, t), lambda h, d: (h, d, 0, 0)),
        out_shape=jax.ShapeDtypeStruct((N_HEADS, 3, t, t), F32),
        compiler_params=pltpu.CompilerParams(
            dimension_semantics=("arbitrary", "arbitrary"), vmem_limit_bytes=VMEM_LIMIT),
        name="bias_tiles",
    )(rel_bias, bucket)


def _attn_kernel(bnd_ref, lq_ref, qt_ref, k_ref, vt_ref, bias_ref, gate_ref, gain_ref, o_ref,
                 q2_scr, m_scr, l_scr, acc_scr, kabs_scr, mreg_scr, sa_scr, sb_scr, *, lambda_init, n_tiles):
    t = qt_ref.shape[-1]
    h = pl.program_id(0)
    i = pl.program_id(1)

    @pl.when(i == 0)
    def _():
        ka = jnp.max(jnp.abs(k_ref[...].astype(F32)), axis=0, keepdims=True)
        kabs_scr[...] = jnp.broadcast_to(ka, (8, HEAD_DIM)).T

    qt = qt_ref[0, 0]
    row = lax.broadcasted_iota(jnp.int32, qt.shape, 0)
    zero = jnp.zeros_like(qt)
    q2_scr[:, :t] = jnp.where(row < HALF_DIM, qt, zero)
    q2_scr[:, t:] = jnp.where(row >= HALF_DIM, qt, zero)
    l_scr[...] = jnp.zeros(l_scr.shape, F32)
    acc_scr[...] = jnp.zeros(acc_scr.shape, F32)

    c_left = bias_ref[0, 0, 0:1, t - 1:t]
    c_right = bias_ref[0, 2, t - 1:t, 0:1]

    wq = jnp.abs(qt.astype(F32)) * kabs_scr[:, 0:1]
    hb = jnp.concatenate([jnp.sum(wq[:HALF_DIM], axis=0, keepdims=True),
                          jnp.sum(wq[HALF_DIM:], axis=0, keepdims=True)], axis=1)
    b_max = bnd_ref[h, 0]
    b_min = bnd_ref[h, 1]
    fixed_ok = 2.0 * jnp.max(hb) + (b_max - b_min) <= EXP2_SPAN

    col_tiles = [slice(c * MXU_COLS, (c + 1) * MXU_COLS) for c in range(2 * t // MXU_COLS)]

    def scores(j, s_ref):
        kt = k_ref[pl.ds(pl.multiple_of(j * t, t), t), :]
        for cs in col_tiles:
            s_ref[:, cs] = jnp.dot(kt, q2_scr[:, cs], preferred_element_type=F32)

    def add_near_bias(j, s_ref):
        @pl.when(jnp.abs(j - i) <= 1)
        def _():
            b = bias_ref[0, j - i + 1]
            s_ref[:, :t] += b
            s_ref[:, t:] += b

    def softmax_pv(j, s_ref):
        region = jnp.where(j < i - 1, 0, jnp.where(j > i + 1, 2, 1))
        vt = vt_ref[0, j]
        for cs in col_tiles:
            p = jnp.exp2(s_ref[:, cs] - mreg_scr[region, :, cs])
            l_scr[:, cs] += jnp.sum(p, axis=0, keepdims=True)
            acc_scr[:, cs] += jnp.dot(vt, p.astype(BF16), preferred_element_type=F32)

    def tile(j, s_cur, s_next):
        add_near_bias(j, s_cur)
        if s_next is not None:
            scores(j + 1, s_next)
        softmax_pv(j, s_cur)

    @pl.when(fixed_ok)
    def _():
        m0 = hb + b_max
        mreg_scr[0] = m0 - c_left
        mreg_scr[1] = m0
        mreg_scr[2] = m0 - c_right

        def pair_body(jj, carry):
            tile(2 * jj, sa_scr, sb_scr)
            tile(2 * jj + 1, sb_scr, sa_scr)
            return carry

        scores(0, sa_scr)
        lax.fori_loop(0, n_tiles // 2 - 1, pair_body, 0)
        tile(n_tiles - 2, sa_scr, sb_scr)
        tile(n_tiles - 1, sb_scr, None)

    def online_step(j, bias):
        kt = k_ref[pl.ds(pl.multiple_of(j * t, t), t), :]
        s = jnp.dot(kt, q2_scr[...], preferred_element_type=F32)
        if bias is not None:
            s = s + jnp.concatenate([bias, bias], axis=1)
        m_old = m_scr[...]
        m_new = jnp.maximum(m_old, jnp.max(s, axis=0, keepdims=True))
        p = jnp.exp2(s - m_new)
        alpha = jnp.exp2(m_old - m_new)
        l_scr[...] = alpha * l_scr[...] + jnp.sum(p, axis=0, keepdims=True)
        pv = jnp.dot(vt_ref[0, j], p.astype(BF16), preferred_element_type=F32)
        acc_scr[...] = alpha * acc_scr[...] + pv
        m_scr[...] = m_new

    @pl.when(jnp.logical_not(fixed_ok))
    def _():
        def far_body(j, carry):
            online_step(j, None)
            return carry

        m_scr[...] = jnp.full(m_scr.shape, NEG_INIT, F32)
        lax.fori_loop(0, i - 1, far_body, 0)
        m_scr[...] = m_scr[...] + c_left
        for d in (-1, 0, 1):
            j = i + d

            @pl.when(jnp.logical_and(j >= 0, j < n_tiles))
            def _(j=j, d=d):
                online_step(j, bias_ref[0, d + 1])

        m_scr[...] = m_scr[...] - c_right
        lax.fori_loop(i + 2, n_tiles, far_body, 0)

    lq = lq_ref[...]
    lam = (jnp.exp(jnp.sum(lq[0:1] * lq[1:2], axis=-1, keepdims=True))
           - jnp.exp(jnp.sum(lq[2:3] * lq[3:4], axis=-1, keepdims=True)) + lambda_init)
    accn = acc_scr[...] * (1.0 / l_scr[...])
    o_t = accn[:, :t] - lam * accn[:, t:]
    o = o_t.T
    ms = jnp.mean(o * o, axis=-1, keepdims=True)
    y = ((o * lax.rsqrt(ms + EPS)) * gain_ref[...]) * (1.0 - lambda_init)
    o_ref[...] = (y * _silu(gate_ref[...].astype(F32))).astype(BF16)


def _attention(bnd, lq, qt4, proj, vt4, bias, gain, lambda_init):
    s = proj.shape[0]
    t = qt4.shape[-1]
    n_tiles = s // t
    kblk = OFF_K // HEAD_DIM
    gblk = OFF_B_GATE // HEAD_DIM
    kern = functools.partial(_attn_kernel, lambda_init=lambda_init, n_tiles=n_tiles)
    return pl.pallas_call(
        kern,
        grid=(N_HEADS, n_tiles),
        in_specs=[
            pl.BlockSpec(memory_space=pltpu.SMEM),
            pl.BlockSpec((4, HALF_DIM), lambda h, i: (0, 0)),
            pl.BlockSpec((1, 1, HEAD_DIM, t), lambda h, i: (h, i, 0, 0)),
            pl.BlockSpec((s, HEAD_DIM), lambda h, i: (0, kblk + h)),
            pl.BlockSpec((1, n_tiles, HEAD_DIM, t), lambda h, i: (h, 0, 0, 0)),
            pl.BlockSpec((1, 3, t, t), lambda h, i: (h, 0, 0, 0)),
            pl.BlockSpec((t, HEAD_DIM), lambda h, i: (i, gblk + h)),
            pl.BlockSpec((1, HEAD_DIM), lambda h, i: (0, h)),
        ],
        out_specs=pl.BlockSpec((t, HEAD_DIM), lambda h, i: (i, h)),
        out_shape=jax.ShapeDtypeStruct((s, D_DIFF), BF16),
        scratch_shapes=[
            pltpu.VMEM((HEAD_DIM, 2 * t), BF16),
            pltpu.VMEM((1, 2 * t), F32),
            pltpu.VMEM((1, 2 * t), F32),
            pltpu.VMEM((HEAD_DIM, 2 * t), F32),
            pltpu.VMEM((HEAD_DIM, 8), F32),
            pltpu.VMEM((3, 1, 2 * t), F32),
            pltpu.VMEM((t, 2 * t), F32),
            pltpu.VMEM((t, 2 * t), F32),
        ],
        compiler_params=pltpu.CompilerParams(
            dimension_semantics=("arbitrary", "arbitrary"), vmem_limit_bytes=VMEM_LIMIT),
        name="diff_attention",
    )(bnd, lq, qt4, proj, vt4, bias, proj, gain)


def _head_transposed(proj, off, t):
    s = proj.shape[0]
    a = proj[:, off:off + D_DIFF].reshape(s // t, t, N_HEADS, HEAD_DIM)
    return a.transpose(2, 0, 3, 1)


def _outproj_kernel(ya_ref, ag_ref, yb_ref, u_ref, vsg_ref, cg_ref, x_ref, wout_ref, ws_ref,
                    bs_ref, vgain_ref, pgain_ref, o_ref, y_scr):
    tm = x_ref.shape[0]
    y_scr[:, 0:D_FOURIER] = (ya_ref[...] * _silu(ag_ref[...].astype(F32))).astype(BF16)
    y_scr[:, D_FOURIER:D_FOURIER + D_DIFF] = yb_ref[...]
    for g in range(D_GMLP // GROUP):
        cols = slice(g * GROUP, (g + 1) * GROUP)
        v = vsg_ref[:, cols].astype(F32)
        ms = jnp.mean(v * v, axis=-1, keepdims=True)
        vn = ((v * lax.rsqrt(ms + EPS)) * vgain_ref[:, cols]).astype(BF16)
        w = ws_ref[g]
        b = bs_ref[g]
        for c in range(tm // CHUNK):
            rows = slice(c * CHUNK, (c + 1) * CHUNK)
            mixed = jnp.dot(w, vn[rows], preferred_element_type=F32) + b
            gate = cg_ref[rows, cols].astype(F32)
            yc = (u_ref[rows, cols].astype(F32) * mixed) * _silu(gate)
            y_scr[rows, D_FOURIER + D_DIFF + g * GROUP:D_FOURIER + D_DIFF + (g + 1) * GROUP] = (
                yc.astype(BF16))
    y = jnp.dot(y_scr[...], wout_ref[...], preferred_element_type=F32)
    ms = jnp.mean(y * y, axis=-1, keepdims=True)
    o_ref[...] = x_ref[...] + (y * lax.rsqrt(ms + EPS)) * pgain_ref[...]


def _outproj(ya, yb, proj, x2, wout_bf, ws_bf, bs3, vgain, pgain):
    s, d = x2.shape
    tm = TM_OUT
    ablk = OFF_A_GATE // D_FOURIER
    ublk, vblk, cblk = OFF_U // D_GMLP, OFF_VSG // D_GMLP, OFF_C_GATE // D_GMLP
    const = lambda shape: pl.BlockSpec(shape, lambda i: tuple(0 for _ in shape))
    return pl.pallas_call(
        _outproj_kernel,
        grid=(s // tm,),
        in_specs=[
            pl.BlockSpec((tm, D_FOURIER), lambda i: (i, 0)),
            pl.BlockSpec((tm, D_FOURIER), lambda i: (i, ablk)),
            pl.BlockSpec((tm, D_DIFF), lambda i: (i, 0)),
            pl.BlockSpec((tm, D_GMLP), lambda i: (i, ublk)),
            pl.BlockSpec((tm, D_GMLP), lambda i: (i, vblk)),
            pl.BlockSpec((tm, D_GMLP), lambda i: (i, cblk)),
            pl.BlockSpec((tm, d), lambda i: (i, 0)),
            const((d, d)),
            const(ws_bf.shape),
            const(bs3.shape),
            const((1, D_GMLP)),
            const((1, d)),
        ],
        out_specs=pl.BlockSpec((tm, d), lambda i: (i, 0)),
        out_shape=jax.ShapeDtypeStruct((s, d), F32),
        scratch_shapes=[pltpu.VMEM((tm, d), BF16)],
        compiler_params=pltpu.CompilerParams(
            dimension_semantics=("arbitrary",), vmem_limit_bytes=VMEM_LIMIT),
        name="outproj",
    )(ya, proj, yb, proj, proj, proj, x2, wout_bf, ws_bf, bs3, vgain, pgain)


def kernel(x, w_in, pre_gain, post_gain, w_fourier, lambda_qk, diff_out_gain, sg_v_gain,
           w_spatial, b_spatial, w_out, rel_bias):
    b, s, d = x.shape
    depth = w_in.shape[0]
    assert b == 1 and w_in.shape[2] == D_IN
    x2 = x.reshape(s, d)

    q_scale = (HALF_DIM ** -0.5) * LOG2E
    colscale = jnp.ones((1, D_IN), F32).at[:, OFF_Q:OFF_Q + D_DIFF].set(q_scale)
    bias = _bias_tiles(rel_bias, T_ATT)
    bias_bounds = jnp.stack([jnp.max(rel_bias, axis=0), jnp.min(rel_bias, axis=0)], axis=1) * LOG2E
    consts = _dft_consts(s // FFT_N2, FFT_N2, GROUP)

    for l in range(depth):
        lambda_init = 0.8 - 0.6 * math.exp(-0.3 * l)
        proj, a_in = _inproj(x2, pre_gain[l].reshape(1, d), w_in[l].astype(BF16), colscale)
        ya = _fourier_branch(a_in, w_fourier[l].astype(BF16), consts)
        qt4 = _head_transposed(proj, OFF_Q, T_ATT)
        vt4 = _head_transposed(proj, OFF_V, T_ATT)
        yb = _attention(bias_bounds, lambda_qk[l], qt4, proj, vt4, bias,
                        diff_out_gain[l].reshape(1, D_DIFF), lambda_init)
        x2 = _outproj(ya, yb, proj, x2, w_out[l].astype(BF16), w_spatial[l].astype(BF16),
                      b_spatial[l].reshape(D_GMLP // GROUP, CHUNK, 1),
                      sg_v_gain[l].reshape(1, D_GMLP), post_gain[l].reshape(1, d))
    return x2.reshape(b, s, d)
```

```python
import functools
import math

import numpy as np
import jax
import jax.numpy as jnp
from jax import lax
from jax.experimental import pallas as pl
from jax.experimental.pallas import tpu as pltpu

F32 = jnp.float32
BF16 = jnp.bfloat16

EPS = 1e-6
LOG2E = math.log2(math.e)
NEG_INIT = -1e30
EXP2_SPAN = 120.0

D_FOURIER = 512
D_DIFF = 1024
D_GMLP = 512
GROUP = 128
HEAD_DIM = 128
HALF_DIM = 64
N_HEADS = 8
CHUNK = 128
N_BUCKETS = 32
MAX_DISTANCE = 128
OFF_A_IN, OFF_A_GATE, OFF_Q, OFF_K, OFF_V, OFF_B_GATE, OFF_U, OFF_VSG, OFF_C_GATE = (
    0, 512, 1024, 2048, 3072, 4096, 5120, 5632, 6144)
D_IN = 6656

MXU_COLS = 256
TM_IN = 512
TN_IN = 1664
TM_OUT = 512
T_ATT = 512
TQ_ATT = 1024
N_BIAS_TILES = 4
ZERO_BIAS_TILE = 3
FFT_N2 = 128
FFT_ROWS = 8
VMEM_LIMIT = 56 * 1024 * 1024


def _silu(x):
    return x * (1.0 / (1.0 + jnp.exp(-x)))


def _inproj_kernel(x_ref, g_ref, w_ref, cs_ref, proj_ref, ain_ref, a_scr):
    j = pl.program_id(1)

    @pl.when(j == 0)
    def _():
        x = x_ref[...]
        ms = jnp.mean(x * x, axis=-1, keepdims=True)
        a_scr[...] = ((x * lax.rsqrt(ms + EPS)) * g_ref[...]).astype(BF16)

    acc = jnp.dot(a_scr[...], w_ref[...], preferred_element_type=F32)
    proj_ref[...] = (acc * cs_ref[...]).astype(BF16)

    @pl.when(j == 0)
    def _():
        ain_ref[...] = acc[:, :D_FOURIER]


def _inproj(x2, gain, w_bf, colscale):
    s, d = x2.shape
    n = w_bf.shape[1]
    grid = (s // TM_IN, n // TN_IN)
    return pl.pallas_call(
        _inproj_kernel,
        grid=grid,
        in_specs=[
            pl.BlockSpec((TM_IN, d), lambda i, j: (i, 0)),
            pl.BlockSpec((1, d), lambda i, j: (0, 0)),
            pl.BlockSpec((d, TN_IN), lambda i, j: (0, j)),
            pl.BlockSpec((1, TN_IN), lambda i, j: (0, j)),
        ],
        out_specs=[
            pl.BlockSpec((TM_IN, TN_IN), lambda i, j: (i, j)),
            pl.BlockSpec((TM_IN, D_FOURIER), lambda i, j: (i, 0)),
        ],
        out_shape=[
            jax.ShapeDtypeStruct((s, n), BF16),
            jax.ShapeDtypeStruct((s, D_FOURIER), F32),
        ],
        scratch_shapes=[pltpu.VMEM((TM_IN, d), BF16)],
        compiler_params=pltpu.CompilerParams(
            dimension_semantics=("arbitrary", "arbitrary"),
            vmem_limit_bytes=VMEM_LIMIT),
        name="inproj",
    )(x2, gain, w_bf, colscale)


def _dft_consts(n1, n2, c):
    def cs(n):
        idx = np.arange(n)
        ang = 2.0 * np.pi * ((idx[:, None] * idx[None, :]) % n) / n
        return np.cos(ang), np.sin(ang)

    c1, s1 = cs(n1)
    c2, s2 = cs(n2)
    cc, sc = cs(c)
    f1 = np.concatenate([c1, -s1], axis=0)
    f2 = np.block([[c2, s2], [s2, -c2]])
    n = n1 * n2
    k1 = np.arange(n1)[:, None]
    m2 = np.arange(n2)[None, :]
    ang = 2.0 * np.pi * ((k1 * m2) % n) / n
    tw_c = np.cos(ang)[:, :, None]
    tw_s = np.sin(ang)[:, :, None]
    groups = D_FOURIER // c
    bdc = np.kron(np.eye(groups), cc)
    bds = -np.kron(np.eye(groups), sc)
    f32 = lambda a: jnp.asarray(a.astype(np.float32))
    return f32(f1), f32(f2), f32(tw_c), f32(tw_s), f32(bdc), f32(bds)


def _fft1_kernel(f1_ref, x_ref, y_ref):
    f1 = f1_ref[...].astype(BF16)
    for r in range(x_ref.shape[1]):
        xr = x_ref[:, r, :].astype(BF16)
        y_ref[:, r, :] = jnp.dot(f1, xr, preferred_element_type=F32)


def _fft1(f1, x3):
    n1, n2, c = x3.shape
    return pl.pallas_call(
        _fft1_kernel,
        grid=(n2 // FFT_ROWS,),
        in_specs=[pl.BlockSpec((2 * n1, n1), lambda j: (0, 0)),
                  pl.BlockSpec((n1, FFT_ROWS, c), lambda j: (0, j, 0))],
        out_specs=pl.BlockSpec((2 * n1, FFT_ROWS, c), lambda j: (0, j, 0)),
        out_shape=jax.ShapeDtypeStruct((2 * n1, n2, c), F32),
        compiler_params=pltpu.CompilerParams(
            dimension_semantics=("arbitrary",), vmem_limit_bytes=VMEM_LIMIT),
        name="fft_stage1",
    )(f1, x3)


def _fft2_kernel(y_ref, twc_ref, tws_ref, f2_ref, bdc_ref, bds_ref, wf_ref, o_ref, *, scale):
    f2 = f2_ref[...].astype(BF16)
    bdc = bdc_ref[...].astype(BF16)
    bds = bds_ref[...].astype(BF16)
    n2 = y_ref.shape[2]
    for r in range(y_ref.shape[1]):
        yr = y_ref[0, r]
        yi = y_ref[1, r]
        tc = twc_ref[r]
        ts = tws_ref[r]
        zr = yr * tc + yi * ts
        zi = yi * tc - yr * ts
        z = jnp.concatenate([zr, zi], axis=0).astype(BF16)
        pq = jnp.dot(f2, z, preferred_element_type=F32)
        p = pq[:n2].astype(BF16)
        q = pq[n2:].astype(BF16)
        f = (jnp.dot(p, bdc, preferred_element_type=F32)
             + jnp.dot(q, bds, preferred_element_type=F32)) * scale
        o_ref[:, r, :] = jnp.dot(f.astype(BF16), wf_ref[...], preferred_element_type=F32)


def _fft2(y4, twc, tws, f2, bdc, bds, wf_bf, scale):
    _, n1, n2, c = y4.shape
    const = lambda shape: pl.BlockSpec(shape, lambda k: tuple(0 for _ in shape))
    return pl.pallas_call(
        functools.partial(_fft2_kernel, scale=scale),
        grid=(n1 // FFT_ROWS,),
        in_specs=[
            pl.BlockSpec((2, FFT_ROWS, n2, c), lambda k: (0, k, 0, 0)),
            pl.BlockSpec((FFT_ROWS, n2, 1), lambda k: (k, 0, 0)),
            pl.BlockSpec((FFT_ROWS, n2, 1), lambda k: (k, 0, 0)),
            const((2 * n2, 2 * n2)),
            const((c, c)),
            const((c, c)),
            const((c, c)),
        ],
        out_specs=pl.BlockSpec((n2, FFT_ROWS, c), lambda k: (0, k, 0)),
        out_shape=jax.ShapeDtypeStruct((n2, n1, c), F32),
        compiler_params=pltpu.CompilerParams(
            dimension_semantics=("arbitrary",), vmem_limit_bytes=VMEM_LIMIT),
        name="fft_stage2",
    )(y4, twc, tws, f2, bdc, bds, wf_bf)


def _fourier_branch(a_in, wf_bf, consts):
    s, c = a_in.shape
    n2 = FFT_N2
    n1 = s // n2
    f1, f2, twc, tws, bdc, bds = consts
    y = _fft1(f1, a_in.reshape(n1, n2, c))
    scale = 1.0 / math.sqrt(s * GROUP)
    out = _fft2(y.reshape(2, n1, n2, c), twc, tws, f2, bdc, bds, wf_bf, scale)
    return out.reshape(s, c)


def _t5_bucket(rel):
    nb = N_BUCKETS // 2
    ret = (rel > 0).astype(jnp.int32) * nb
    n = jnp.abs(rel)
    max_exact = nb // 2
    nf = jnp.maximum(n, 1).astype(jnp.float32)
    large = max_exact + (jnp.log(nf / max_exact) / math.log(MAX_DISTANCE / max_exact)
                         * (nb - max_exact)).astype(jnp.int32)
    large = jnp.minimum(large, nb - 1)
    return ret + jnp.where(n < max_exact, n, large)


def _bias_kernel(rb_ref, bk_ref, o_ref):
    h = pl.program_id(0)
    bk = bk_ref[0]
    acc = jnp.zeros(bk.shape, F32)
    for b in range(N_BUCKETS):
        acc = jnp.where(bk == b, rb_ref[b, h], acc)
    o_ref[0, 0] = acc * LOG2E


def _bias_tiles(rel_bias, t):
    kk = jnp.arange(t, dtype=jnp.int32)[:, None]
    rr = jnp.arange(t, dtype=jnp.int32)[None, :]
    rel = jnp.stack([d * t + kk - rr for d in (-1, 0, 1)])
    bucket = jnp.concatenate([_t5_bucket(rel), jnp.full((1, t, t), -1, jnp.int32)])
    return pl.pallas_call(
        _bias_kernel,
        grid=(N_HEADS, N_BIAS_TILES),
        in_specs=[pl.BlockSpec(memory_space=pltpu.SMEM),
                  pl.BlockSpec((1, t, t), lambda h, d: (d, 0, 0))],
        out_specs=pl.BlockSpec((1, 1, t, t), lambda h, d: (h, d, 0, 0)),
        out_shape=jax.ShapeDtypeStruct((N_HEADS, N_BIAS_TILES, t, t), F32),
        compiler_params=pltpu.CompilerParams(
            dimension_semantics=("arbitrary", "arbitrary"), vmem_limit_bytes=VMEM_LIMIT),
        name="bias_tiles",
    )(rel_bias, bucket)


def _attn_kernel(bnd_ref, lq_ref, qt_ref, k_ref, vt_ref, bias_ref, gate_ref, gain_ref, o_ref,
                 q2_scr, m_scr, l_scr, acc_scr, kabs_scr, mreg_scr, sa_scr, sb_scr,
                 *, lambda_init, n_tiles):
    tq = qt_ref.shape[-1]
    t = vt_ref.shape[-1]
    nqb = tq // t
    h = pl.program_id(0)
    i = pl.program_id(1)

    @pl.when(i == 0)
    def _():
        ka = jnp.max(jnp.abs(k_ref[...].astype(F32)), axis=0, keepdims=True)
        kabs_scr[...] = jnp.broadcast_to(ka, (8, HEAD_DIM)).T

    qt = qt_ref[0, 0]
    row = lax.broadcasted_iota(jnp.int32, qt.shape, 0)
    zero = jnp.zeros_like(qt)
    q2_scr[:, :tq] = jnp.where(row < HALF_DIM, qt, zero)
    q2_scr[:, tq:] = jnp.where(row >= HALF_DIM, qt, zero)
    l_scr[...] = jnp.zeros(l_scr.shape, F32)
    acc_scr[...] = jnp.zeros(acc_scr.shape, F32)

    c_left = bias_ref[0, 0, 0:1, t - 1:t]
    c_right = bias_ref[0, 2, t - 1:t, 0:1]

    wq = jnp.abs(qt.astype(F32)) * kabs_scr[:, 0:1]
    hb = jnp.concatenate([jnp.sum(wq[:HALF_DIM], axis=0, keepdims=True),
                          jnp.sum(wq[HALF_DIM:], axis=0, keepdims=True)], axis=1)
    b_max = bnd_ref[h, 0]
    b_min = bnd_ref[h, 1]
    fixed_ok = 2.0 * jnp.max(hb) + (b_max - b_min) <= EXP2_SPAN

    def block_relation(j, col0):
        e = j - (nqb * i + (col0 % tq) // t)
        tile_idx = jnp.where(jnp.abs(e) <= 1, e + 1, ZERO_BIAS_TILE)
        region = jnp.where(e < -1, 0, jnp.where(e > 1, 2, 1))
        return tile_idx, region

    def scores(j, s_ref):
        kt = k_ref[pl.ds(pl.multiple_of(j * t, t), t), :]
        for c0 in range(0, 2 * tq, MXU_COLS):
            cs = slice(c0, c0 + MXU_COLS)
            s_ref[:, cs] = jnp.dot(kt, q2_scr[:, cs], preferred_element_type=F32)

    def softmax_pv(j, s_ref):
        vt = vt_ref[0, j]
        for c0 in range(0, 2 * tq, MXU_COLS):
            cs = slice(c0, c0 + MXU_COLS)
            tile_idx, region = block_relation(j, c0)
            b0 = c0 % t
            s = s_ref[:, cs] + bias_ref[0, tile_idx, :, b0:b0 + MXU_COLS]
            p = jnp.exp2(s - mreg_scr[region, :, cs])
            l_scr[:, cs] += jnp.sum(p, axis=0, keepdims=True)
            acc_scr[:, cs] += jnp.dot(vt, p.astype(BF16), preferred_element_type=F32)

    def tile(j, s_cur, s_next):
        if s_next is not None:
            scores(j + 1, s_next)
        softmax_pv(j, s_cur)

    @pl.when(fixed_ok)
    def _():
        m0 = hb + b_max
        mreg_scr[0] = m0 - c_left
        mreg_scr[1] = m0
        mreg_scr[2] = m0 - c_right

        def pair_body(jj, carry):
            tile(2 * jj, sa_scr, sb_scr)
            tile(2 * jj + 1, sb_scr, sa_scr)
            return carry

        scores(0, sa_scr)
        lax.fori_loop(0, n_tiles // 2 - 1, pair_body, 0)
        tile(n_tiles - 2, sa_scr, sb_scr)
        tile(n_tiles - 1, sb_scr, None)

    def online_step(j):
        kt = k_ref[pl.ds(pl.multiple_of(j * t, t), t), :]
        vt = vt_ref[0, j]
        for c0 in range(0, 2 * tq, t):
            cs = slice(c0, c0 + t)
            tile_idx, region = block_relation(j, c0)
            c_far = jnp.where(region == 0, c_left, jnp.where(region == 2, c_right, 0.0))
            s = (jnp.dot(kt, q2_scr[:, cs], preferred_element_type=F32)
                 + (bias_ref[0, tile_idx] + c_far))
            m_old = m_scr[:, cs]
            m_new = jnp.maximum(m_old, jnp.max(s, axis=0, keepdims=True))
            p = jnp.exp2(s - m_new)
            alpha = jnp.exp2(m_old - m_new)
            l_scr[:, cs] = alpha * l_scr[:, cs] + jnp.sum(p, axis=0, keepdims=True)
            pv = jnp.dot(vt, p.astype(BF16), preferred_element_type=F32)
            acc_scr[:, cs] = alpha * acc_scr[:, cs] + pv
            m_scr[:, cs] = m_new

    @pl.when(jnp.logical_not(fixed_ok))
    def _():
        def body(j, carry):
            online_step(j)
            return carry

        m_scr[...] = jnp.full(m_scr.shape, NEG_INIT, F32)
        lax.fori_loop(0, n_tiles, body, 0)

    lq = lq_ref[...]
    lam = (jnp.exp(jnp.sum(lq[0:1] * lq[1:2], axis=-1, keepdims=True))
           - jnp.exp(jnp.sum(lq[2:3] * lq[3:4], axis=-1, keepdims=True)) + lambda_init)
    accn = acc_scr[...] * (1.0 / l_scr[...])
    o_t = accn[:, :tq] - lam * accn[:, tq:]
    o = o_t.T
    ms = jnp.mean(o * o, axis=-1, keepdims=True)
    y = ((o * lax.rsqrt(ms + EPS)) * gain_ref[...]) * (1.0 - lambda_init)
    o_ref[...] = (y * _silu(gate_ref[...].astype(F32))).astype(BF16)


def _attention(bnd, lq, qt4, proj, vt4, bias, gain, lambda_init):
    s = proj.shape[0]
    tq = qt4.shape[-1]
    t = vt4.shape[-1]
    n_tiles = s // t
    kblk = OFF_K // HEAD_DIM
    gblk = OFF_B_GATE // HEAD_DIM
    kern = functools.partial(_attn_kernel, lambda_init=lambda_init, n_tiles=n_tiles)
    return pl.pallas_call(
        kern,
        grid=(N_HEADS, s // tq),
        in_specs=[
            pl.BlockSpec(memory_space=pltpu.SMEM),
            pl.BlockSpec((4, HALF_DIM), lambda h, i: (0, 0)),
            pl.BlockSpec((1, 1, HEAD_DIM, tq), lambda h, i: (h, i, 0, 0)),
            pl.BlockSpec((s, HEAD_DIM), lambda h, i: (0, kblk + h)),
            pl.BlockSpec((1, n_tiles, HEAD_DIM, t), lambda h, i: (h, 0, 0, 0)),
            pl.BlockSpec((1, N_BIAS_TILES, t, t), lambda h, i: (h, 0, 0, 0)),
            pl.BlockSpec((tq, HEAD_DIM), lambda h, i: (i, gblk + h)),
            pl.BlockSpec((1, HEAD_DIM), lambda h, i: (0, h)),
        ],
        out_specs=pl.BlockSpec((tq, HEAD_DIM), lambda h, i: (i, h)),
        out_shape=jax.ShapeDtypeStruct((s, D_DIFF), BF16),
        scratch_shapes=[
            pltpu.VMEM((HEAD_DIM, 2 * tq), BF16),
            pltpu.VMEM((1, 2 * tq), F32),
            pltpu.VMEM((1, 2 * tq), F32),
            pltpu.VMEM((HEAD_DIM, 2 * tq), F32),
            pltpu.VMEM((HEAD_DIM, 8), F32),
            pltpu.VMEM((3, 1, 2 * tq), F32),
            pltpu.VMEM((t, 2 * tq), F32),
            pltpu.VMEM((t, 2 * tq), F32),
        ],
        compiler_params=pltpu.CompilerParams(
            dimension_semantics=("arbitrary", "arbitrary"), vmem_limit_bytes=VMEM_LIMIT),
        name="diff_attention",
    )(bnd, lq, qt4, proj, vt4, bias, proj, gain)


def _head_transposed(proj, off, t):
    s = proj.shape[0]
    a = proj[:, off:off + D_DIFF].reshape(s // t, t, N_HEADS, HEAD_DIM)
    return a.transpose(2, 0, 3, 1)


def _outproj_kernel(ya_ref, ag_ref, yb_ref, u_ref, vsg_ref, cg_ref, x_ref, wout_ref, ws_ref,
                    bs_ref, vgain_ref, pgain_ref, o_ref, y_scr):
    tm = x_ref.shape[0]
    y_scr[:, 0:D_FOURIER] = (ya_ref[...] * _silu(ag_ref[...].astype(F32))).astype(BF16)
    y_scr[:, D_FOURIER:D_FOURIER + D_DIFF] = yb_ref[...]
    for g in range(D_GMLP // GROUP):
        cols = slice(g * GROUP, (g + 1) * GROUP)
        v = vsg_ref[:, cols].astype(F32)
        ms = jnp.mean(v * v, axis=-1, keepdims=True)
        vn = ((v * lax.rsqrt(ms + EPS)) * vgain_ref[:, cols]).astype(BF16)
        w = ws_ref[g]
        b = bs_ref[g]
        for c in range(tm // CHUNK):
            rows = slice(c * CHUNK, (c + 1) * CHUNK)
            mixed = jnp.dot(w, vn[rows], preferred_element_type=F32) + b
            gate = cg_ref[rows, cols].astype(F32)
            yc = (u_ref[rows, cols].astype(F32) * mixed) * _silu(gate)
            y_scr[rows, D_FOURIER + D_DIFF + g * GROUP:D_FOURIER + D_DIFF + (g + 1) * GROUP] = (
                yc.astype(BF16))
    y = jnp.dot(y_scr[...], wout_ref[...], preferred_element_type=F32)
    ms = jnp.mean(y * y, axis=-1, keepdims=True)
    o_ref[...] = x_ref[...] + (y * lax.rsqrt(ms + EPS)) * pgain_ref[...]


def _outproj(ya, yb, proj, x2, wout_bf, ws_bf, bs3, vgain, pgain):
    s, d = x2.shape
    tm = TM_OUT
    ablk = OFF_A_GATE // D_FOURIER
    ublk, vblk, cblk = OFF_U // D_GMLP, OFF_VSG // D_GMLP, OFF_C_GATE // D_GMLP
    const = lambda shape: pl.BlockSpec(shape, lambda i: tuple(0 for _ in shape))
    return pl.pallas_call(
        _outproj_kernel,
        grid=(s // tm,),
        in_specs=[
            pl.BlockSpec((tm, D_FOURIER), lambda i: (i, 0)),
            pl.BlockSpec((tm, D_FOURIER), lambda i: (i, ablk)),
            pl.BlockSpec((tm, D_DIFF), lambda i: (i, 0)),
            pl.BlockSpec((tm, D_GMLP), lambda i: (i, ublk)),
            pl.BlockSpec((tm, D_GMLP), lambda i: (i, vblk)),
            pl.BlockSpec((tm, D_GMLP), lambda i: (i, cblk)),
            pl.BlockSpec((tm, d), lambda i: (i, 0)),
            const((d, d)),
            const(ws_bf.shape),
            const(bs3.shape),
            const((1, D_GMLP)),
            const((1, d)),
        ],
        out_specs=pl.BlockSpec((tm, d), lambda i: (i, 0)),
        out_shape=jax.ShapeDtypeStruct((s, d), F32),
        scratch_shapes=[pltpu.VMEM((tm, d), BF16)],
        compiler_params=pltpu.CompilerParams(
            dimension_semantics=("arbitrary",), vmem_limit_bytes=VMEM_LIMIT),
        name="outproj",
    )(ya, proj, yb, proj, proj, proj, x2, wout_bf, ws_bf, bs3, vgain, pgain)


def kernel(x, w_in, pre_gain, post_gain, w_fourier, lambda_qk, diff_out_gain, sg_v_gain,
           w_spatial, b_spatial, w_out, rel_bias):
    b, s, d = x.shape
    depth = w_in.shape[0]
    assert b == 1 and w_in.shape[2] == D_IN
    x2 = x.reshape(s, d)

    q_scale = (HALF_DIM ** -0.5) * LOG2E
    colscale = jnp.ones((1, D_IN), F32).at[:, OFF_Q:OFF_Q + D_DIFF].set(q_scale)
    bias = _bias_tiles(rel_bias, T_ATT)
    bias_bounds = jnp.stack([jnp.max(rel_bias, axis=0), jnp.min(rel_bias, axis=0)], axis=1) * LOG2E
    consts = _dft_consts(s // FFT_N2, FFT_N2, GROUP)

    for l in range(depth):
        lambda_init = 0.8 - 0.6 * math.exp(-0.3 * l)
        proj, a_in = _inproj(x2, pre_gain[l].reshape(1, d), w_in[l].astype(BF16), colscale)
        ya = _fourier_branch(a_in, w_fourier[l].astype(BF16), consts)
        qt4 = _head_transposed(proj, OFF_Q, TQ_ATT)
        vt4 = _head_transposed(proj, OFF_V, T_ATT)
        yb = _attention(bias_bounds, lambda_qk[l], qt4, proj, vt4, bias,
                        diff_out_gain[l].reshape(1, D_DIFF), lambda_init)
        x2 = _outproj(ya, yb, proj, x2, w_out[l].astype(BF16), w_spatial[l].astype(BF16),
                      b_spatial[l].reshape(D_GMLP // GROUP, CHUNK, 1),
                      sg_v_gain[l].reshape(1, D_GMLP), post_gain[l].reshape(1, d))
    return x2.reshape(b, s, d)
```

```python
import functools
import math

import numpy as np
import jax
import jax.numpy as jnp
from jax import lax
from jax.experimental import pallas as pl
from jax.experimental.pallas import tpu as pltpu

F32 = jnp.float32
BF16 = jnp.bfloat16

EPS = 1e-6
LOG2E = math.log2(math.e)
NEG_INIT = -1e30
EXP2_SPAN = 120.0

D_FOURIER = 512
D_DIFF = 1024
D_GMLP = 512
GROUP = 128
HEAD_DIM = 128
HALF_DIM = 64
N_HEADS = 8
CHUNK = 128
N_BUCKETS = 32
MAX_DISTANCE = 128
OFF_A_IN, OFF_A_GATE, OFF_Q, OFF_K, OFF_V, OFF_B_GATE, OFF_U, OFF_VSG, OFF_C_GATE = (
    0, 512, 1024, 2048, 3072, 4096, 5120, 5632, 6144)
D_IN = 6656

MXU_COLS = 256
TM_IN = 512
TN_IN = 1664
TM_OUT = 512
T_ATT = 512
TQ_ATT = 1024
N_BIAS_TILES = 4
ZERO_BIAS_TILE = 3
BIAS_COL_SHIFT = 256
FFT_N2 = 128
FFT_ROWS = 8
VMEM_LIMIT = 56 * 1024 * 1024


def _silu(x):
    return x * (1.0 / (1.0 + jnp.exp(-x)))


def _inproj_kernel(x_ref, g_ref, w_ref, cs_ref, proj_ref, ain_ref, a_scr):
    j = pl.program_id(1)

    @pl.when(j == 0)
    def _():
        x = x_ref[...]
        ms = jnp.mean(x * x, axis=-1, keepdims=True)
        a_scr[...] = ((x * lax.rsqrt(ms + EPS)) * g_ref[...]).astype(BF16)

    acc = jnp.dot(a_scr[...], w_ref[...], preferred_element_type=F32)
    proj_ref[...] = (acc * cs_ref[...]).astype(BF16)

    @pl.when(j == 0)
    def _():
        ain_ref[...] = acc[:, :D_FOURIER]


def _inproj(x2, gain, w_bf, colscale):
    s, d = x2.shape
    n = w_bf.shape[1]
    grid = (s // TM_IN, n // TN_IN)
    return pl.pallas_call(
        _inproj_kernel,
        grid=grid,
        in_specs=[
            pl.BlockSpec((TM_IN, d), lambda i, j: (i, 0)),
            pl.BlockSpec((1, d), lambda i, j: (0, 0)),
            pl.BlockSpec((d, TN_IN), lambda i, j: (0, j)),
            pl.BlockSpec((1, TN_IN), lambda i, j: (0, j)),
        ],
        out_specs=[
            pl.BlockSpec((TM_IN, TN_IN), lambda i, j: (i, j)),
            pl.BlockSpec((TM_IN, D_FOURIER), lambda i, j: (i, 0)),
        ],
        out_shape=[
            jax.ShapeDtypeStruct((s, n), BF16),
            jax.ShapeDtypeStruct((s, D_FOURIER), F32),
        ],
        scratch_shapes=[pltpu.VMEM((TM_IN, d), BF16)],
        compiler_params=pltpu.CompilerParams(
            dimension_semantics=("arbitrary", "arbitrary"),
            vmem_limit_bytes=VMEM_LIMIT),
        name="inproj",
    )(x2, gain, w_bf, colscale)


def _dft_consts(n1, n2, c):
    def cs(n):
        idx = np.arange(n)
        ang = 2.0 * np.pi * ((idx[:, None] * idx[None, :]) % n) / n
        return np.cos(ang), np.sin(ang)

    c1, s1 = cs(n1)
    c2, s2 = cs(n2)
    cc, sc = cs(c)
    f1 = np.concatenate([c1, -s1], axis=0)
    f2 = np.block([[c2, s2], [s2, -c2]])
    n = n1 * n2
    k1 = np.arange(n1)[:, None]
    m2 = np.arange(n2)[None, :]
    ang = 2.0 * np.pi * ((k1 * m2) % n) / n
    tw_c = np.cos(ang)[:, :, None]
    tw_s = np.sin(ang)[:, :, None]
    groups = D_FOURIER // c
    bdc = np.kron(np.eye(groups), cc)
    bds = -np.kron(np.eye(groups), sc)
    f32 = lambda a: jnp.asarray(a.astype(np.float32))
    return f32(f1), f32(f2), f32(tw_c), f32(tw_s), f32(bdc), f32(bds)


def _fft1_kernel(f1_ref, x_ref, y_ref):
    f1 = f1_ref[...].astype(BF16)
    for r in range(x_ref.shape[1]):
        xr = x_ref[:, r, :].astype(BF16)
        y_ref[:, r, :] = jnp.dot(f1, xr, preferred_element_type=F32)


def _fft1(f1, x3):
    n1, n2, c = x3.shape
    return pl.pallas_call(
        _fft1_kernel,
        grid=(n2 // FFT_ROWS,),
        in_specs=[pl.BlockSpec((2 * n1, n1), lambda j: (0, 0)),
                  pl.BlockSpec((n1, FFT_ROWS, c), lambda j: (0, j, 0))],
        out_specs=pl.BlockSpec((2 * n1, FFT_ROWS, c), lambda j: (0, j, 0)),
        out_shape=jax.ShapeDtypeStruct((2 * n1, n2, c), F32),
        compiler_params=pltpu.CompilerParams(
            dimension_semantics=("arbitrary",), vmem_limit_bytes=VMEM_LIMIT),
        name="fft_stage1",
    )(f1, x3)


def _fft2_kernel(y_ref, twc_ref, tws_ref, f2_ref, bdc_ref, bds_ref, wf_ref, o_ref, *, scale):
    f2 = f2_ref[...].astype(BF16)
    bdc = bdc_ref[...].astype(BF16)
    bds = bds_ref[...].astype(BF16)
    n2 = y_ref.shape[2]
    for r in range(y_ref.shape[1]):
        yr = y_ref[0, r]
        yi = y_ref[1, r]
        tc = twc_ref[r]
        ts = tws_ref[r]
        zr = yr * tc + yi * ts
        zi = yi * tc - yr * ts
        z = jnp.concatenate([zr, zi], axis=0).astype(BF16)
        pq = jnp.dot(f2, z, preferred_element_type=F32)
        p = pq[:n2].astype(BF16)
        q = pq[n2:].astype(BF16)
        f = (jnp.dot(p, bdc, preferred_element_type=F32)
             + jnp.dot(q, bds, preferred_element_type=F32)) * scale
        o_ref[:, r, :] = jnp.dot(f.astype(BF16), wf_ref[...], preferred_element_type=F32)


def _fft2(y4, twc, tws, f2, bdc, bds, wf_bf, scale):
    _, n1, n2, c = y4.shape
    const = lambda shape: pl.BlockSpec(shape, lambda k: tuple(0 for _ in shape))
    return pl.pallas_call(
        functools.partial(_fft2_kernel, scale=scale),
        grid=(n1 // FFT_ROWS,),
        in_specs=[
            pl.BlockSpec((2, FFT_ROWS, n2, c), lambda k: (0, k, 0, 0)),
            pl.BlockSpec((FFT_ROWS, n2, 1), lambda k: (k, 0, 0)),
            pl.BlockSpec((FFT_ROWS, n2, 1), lambda k: (k, 0, 0)),
            const((2 * n2, 2 * n2)),
            const((c, c)),
            const((c, c)),
            const((c, c)),
        ],
        out_specs=pl.BlockSpec((n2, FFT_ROWS, c), lambda k: (0, k, 0)),
        out_shape=jax.ShapeDtypeStruct((n2, n1, c), F32),
        compiler_params=pltpu.CompilerParams(
            dimension_semantics=("arbitrary",), vmem_limit_bytes=VMEM_LIMIT),
        name="fft_stage2",
    )(y4, twc, tws, f2, bdc, bds, wf_bf)


def _fourier_branch(a_in, wf_bf, consts):
    s, c = a_in.shape
    n2 = FFT_N2
    n1 = s // n2
    f1, f2, twc, tws, bdc, bds = consts
    y = _fft1(f1, a_in.reshape(n1, n2, c))
    scale = 1.0 / math.sqrt(s * GROUP)
    out = _fft2(y.reshape(2, n1, n2, c), twc, tws, f2, bdc, bds, wf_bf, scale)
    return out.reshape(s, c)


def _t5_bucket(rel):
    nb = N_BUCKETS // 2
    ret = (rel > 0).astype(jnp.int32) * nb
    n = jnp.abs(rel)
    max_exact = nb // 2
    nf = jnp.maximum(n, 1).astype(jnp.float32)
    large = max_exact + (jnp.log(nf / max_exact) / math.log(MAX_DISTANCE / max_exact)
                         * (nb - max_exact)).astype(jnp.int32)
    large = jnp.minimum(large, nb - 1)
    return ret + jnp.where(n < max_exact, n, large)


def _bias_kernel(rb_ref, bk_ref, o_ref):
    h = pl.program_id(0)
    bk = bk_ref[0]
    acc = jnp.zeros(bk.shape, F32)
    for b in range(N_BUCKETS):
        acc = jnp.where(bk == b, rb_ref[b, h], acc)
    o_ref[0, 0] = acc * LOG2E


def _bias_tiles(rel_bias, t):
    kk = jnp.arange(t, dtype=jnp.int32)[:, None]
    rr = ((jnp.arange(t, dtype=jnp.int32) - BIAS_COL_SHIFT) % t)[None, :]
    rel = jnp.stack([d * t + kk - rr for d in (-1, 0, 1)])
    bucket = jnp.concatenate([_t5_bucket(rel), jnp.full((1, t, t), -1, jnp.int32)])
    return pl.pallas_call(
        _bias_kernel,
        grid=(N_HEADS, N_BIAS_TILES),
        in_specs=[pl.BlockSpec(memory_space=pltpu.SMEM),
                  pl.BlockSpec((1, t, t), lambda h, d: (d, 0, 0))],
        out_specs=pl.BlockSpec((1, 1, t, t), lambda h, d: (h, d, 0, 0)),
        out_shape=jax.ShapeDtypeStruct((N_HEADS, N_BIAS_TILES, t, t), F32),
        compiler_params=pltpu.CompilerParams(
            dimension_semantics=("arbitrary", "arbitrary"), vmem_limit_bytes=VMEM_LIMIT),
        name="bias_tiles",
    )(rel_bias, bucket)


def _attn_kernel(bnd_ref, lq_ref, qt_ref, k_ref, vt_ref, bias_ref, gate_ref, gain_ref, o_ref,
                 q2_scr, m_scr, l_scr, acc_scr, kabs_scr, mreg_scr, sa_scr, sb_scr,
                 *, lambda_init, n_tiles):
    tq = qt_ref.shape[-1]
    t = vt_ref.shape[-1]
    nqb = tq // t
    h = pl.program_id(0)
    i = pl.program_id(1)

    @pl.when(i == 0)
    def _():
        ka = jnp.max(jnp.abs(k_ref[...].astype(F32)), axis=0, keepdims=True)
        kabs_scr[...] = jnp.broadcast_to(ka, (8, HEAD_DIM)).T

    qt = qt_ref[0, 0]
    row = lax.broadcasted_iota(jnp.int32, qt.shape, 0)
    zero = jnp.zeros_like(qt)
    q2_scr[:, :tq] = jnp.where(row < HALF_DIM, qt, zero)
    q2_scr[:, tq:] = jnp.where(row >= HALF_DIM, qt, zero)
    l_scr[...] = jnp.zeros(l_scr.shape, F32)
    acc_scr[...] = jnp.zeros(acc_scr.shape, F32)

    cl_col = (t - 1 + BIAS_COL_SHIFT) % t
    cr_col = BIAS_COL_SHIFT % t
    c_left = bias_ref[0, 0, 0:1, cl_col:cl_col + 1]
    c_right = bias_ref[0, 2, t - 1:t, cr_col:cr_col + 1]

    wq = jnp.abs(qt.astype(F32)) * kabs_scr[:, 0:1]
    hb = jnp.concatenate([jnp.sum(wq[:HALF_DIM], axis=0, keepdims=True),
                          jnp.sum(wq[HALF_DIM:], axis=0, keepdims=True)], axis=1)
    b_max = bnd_ref[h, 0]
    b_min = bnd_ref[h, 1]
    fixed_ok = 2.0 * jnp.max(hb) + (b_max - b_min) <= EXP2_SPAN

    def block_relation(j, col0):
        e = j - (nqb * i + (col0 % tq) // t)
        tile_idx = jnp.where(jnp.abs(e) <= 1, e + 1, ZERO_BIAS_TILE)
        region = jnp.where(e < -1, 0, jnp.where(e > 1, 2, 1))
        return tile_idx, region

    def scores(j, s_ref):
        kt = k_ref[pl.ds(pl.multiple_of(j * t, t), t), :]
        for c0 in range(0, 2 * tq, MXU_COLS):
            cs = slice(c0, c0 + MXU_COLS)
            s_ref[:, cs] = jnp.dot(kt, q2_scr[:, cs], preferred_element_type=F32)

    def softmax_pv(j, s_ref, with_bias):
        vt = vt_ref[0, j]
        for c0 in range(0, 2 * tq, MXU_COLS):
            cs = slice(c0, c0 + MXU_COLS)
            tile_idx, region = block_relation(j, c0)
            s = s_ref[:, cs]
            if with_bias:
                b0 = (c0 + BIAS_COL_SHIFT) % t
                s = s + bias_ref[0, tile_idx, :, b0:b0 + MXU_COLS]
            p = jnp.exp2(s - mreg_scr[region, :, cs])
            l_scr[:, cs] += jnp.sum(p, axis=0, keepdims=True)
            acc_scr[:, cs] += jnp.dot(vt, p.astype(BF16), preferred_element_type=F32)

    def tile(j, s_cur, s_next, with_bias):
        if s_next is not None:
            scores(j + 1, s_next)
        softmax_pv(j, s_cur, with_bias)

    def tile_pair(j, last):
        def run(with_bias):
            tile(j, sa_scr, sb_scr, with_bias)
            tile(j + 1, sb_scr, None if last else sa_scr, with_bias)

        near = jnp.logical_and(j + 1 >= nqb * i - 1, j <= nqb * i + nqb)
        lax.cond(near, lambda: run(True), lambda: run(False))

    @pl.when(fixed_ok)
    def _():
        m0 = hb + b_max
        mreg_scr[0] = m0 - c_left
        mreg_scr[1] = m0
        mreg_scr[2] = m0 - c_right

        def pair_body(jj, carry):
            tile_pair(2 * jj, last=False)
            return carry

        scores(0, sa_scr)
        lax.fori_loop(0, n_tiles // 2 - 1, pair_body, 0)
        tile_pair(n_tiles - 2, last=True)

    def online_step(j):
        kt = k_ref[pl.ds(pl.multiple_of(j * t, t), t), :]
        vt = vt_ref[0, j]
        for c0 in range(0, 2 * tq, t):
            cs = slice(c0, c0 + t)
            tile_idx, region = block_relation(j, c0)
            c_far = jnp.where(region == 0, c_left, jnp.where(region == 2, c_right, 0.0))
            b = bias_ref[0, tile_idx]
            b = jnp.concatenate([b[:, BIAS_COL_SHIFT:], b[:, :BIAS_COL_SHIFT]], axis=1)
            s = jnp.dot(kt, q2_scr[:, cs], preferred_element_type=F32) + (b + c_far)
            m_old = m_scr[:, cs]
            m_new = jnp.maximum(m_old, jnp.max(s, axis=0, keepdims=True))
            p = jnp.exp2(s - m_new)
            alpha = jnp.exp2(m_old - m_new)
            l_scr[:, cs] = alpha * l_scr[:, cs] + jnp.sum(p, axis=0, keepdims=True)
            pv = jnp.dot(vt, p.astype(BF16), preferred_element_type=F32)
            acc_scr[:, cs] = alpha * acc_scr[:, cs] + pv
            m_scr[:, cs] = m_new

    @pl.when(jnp.logical_not(fixed_ok))
    def _():
        def body(j, carry):
            online_step(j)
            return carry

        m_scr[...] = jnp.full(m_scr.shape, NEG_INIT, F32)
        lax.fori_loop(0, n_tiles, body, 0)

    lq = lq_ref[...]
    lam = (jnp.exp(jnp.sum(lq[0:1] * lq[1:2], axis=-1, keepdims=True))
           - jnp.exp(jnp.sum(lq[2:3] * lq[3:4], axis=-1, keepdims=True)) + lambda_init)
    accn = acc_scr[...] * (1.0 / l_scr[...])
    o_t = accn[:, :tq] - lam * accn[:, tq:]
    o = o_t.T
    ms = jnp.mean(o * o, axis=-1, keepdims=True)
    y = ((o * lax.rsqrt(ms + EPS)) * gain_ref[...]) * (1.0 - lambda_init)
    o_ref[...] = (y * _silu(gate_ref[...].astype(F32))).astype(BF16)


def _attention(bnd, lq, qt4, proj, vt4, bias, gain, lambda_init):
    s = proj.shape[0]
    tq = qt4.shape[-1]
    t = vt4.shape[-1]
    n_tiles = s // t
    kblk = OFF_K // HEAD_DIM
    gblk = OFF_B_GATE // HEAD_DIM
    kern = functools.partial(_attn_kernel, lambda_init=lambda_init, n_tiles=n_tiles)
    return pl.pallas_call(
        kern,
        grid=(N_HEADS, s // tq),
        in_specs=[
            pl.BlockSpec(memory_space=pltpu.SMEM),
            pl.BlockSpec((4, HALF_DIM), lambda h, i: (0, 0)),
            pl.BlockSpec((1, 1, HEAD_DIM, tq), lambda h, i: (h, i, 0, 0)),
            pl.BlockSpec((s, HEAD_DIM), lambda h, i: (0, kblk + h)),
            pl.BlockSpec((1, n_tiles, HEAD_DIM, t), lambda h, i: (h, 0, 0, 0)),
            pl.BlockSpec((1, N_BIAS_TILES, t, t), lambda h, i: (h, 0, 0, 0)),
            pl.BlockSpec((tq, HEAD_DIM), lambda h, i: (i, gblk + h)),
            pl.BlockSpec((1, HEAD_DIM), lambda h, i: (0, h)),
        ],
        out_specs=pl.BlockSpec((tq, HEAD_DIM), lambda h, i: (i, h)),
        out_shape=jax.ShapeDtypeStruct((s, D_DIFF), BF16),
        scratch_shapes=[
            pltpu.VMEM((HEAD_DIM, 2 * tq), BF16),
            pltpu.VMEM((1, 2 * tq), F32),
            pltpu.VMEM((1, 2 * tq), F32),
            pltpu.VMEM((HEAD_DIM, 2 * tq), F32),
            pltpu.VMEM((HEAD_DIM, 8), F32),
            pltpu.VMEM((3, 1, 2 * tq), F32),
            pltpu.VMEM((t, 2 * tq), F32),
            pltpu.VMEM((t, 2 * tq), F32),
        ],
        compiler_params=pltpu.CompilerParams(
            dimension_semantics=("arbitrary", "arbitrary"), vmem_limit_bytes=VMEM_LIMIT),
        name="diff_attention",
    )(bnd, lq, qt4, proj, vt4, bias, proj, gain)


def _head_transposed(proj, off, t):
    s = proj.shape[0]
    a = proj[:, off:off + D_DIFF].reshape(s // t, t, N_HEADS, HEAD_DIM)
    return a.transpose(2, 0, 3, 1)


def _outproj_kernel(ya_ref, ag_ref, yb_ref, u_ref, vsg_ref, cg_ref, x_ref, wout_ref, ws_ref,
                    bs_ref, vgain_ref, pgain_ref, o_ref, y_scr):
    tm = x_ref.shape[0]
    y_scr[:, 0:D_FOURIER] = (ya_ref[...] * _silu(ag_ref[...].astype(F32))).astype(BF16)
    y_scr[:, D_FOURIER:D_FOURIER + D_DIFF] = yb_ref[...]
    for g in range(D_GMLP // GROUP):
        cols = slice(g * GROUP, (g + 1) * GROUP)
        v = vsg_ref[:, cols].astype(F32)
        ms = jnp.mean(v * v, axis=-1, keepdims=True)
        vn = ((v * lax.rsqrt(ms + EPS)) * vgain_ref[:, cols]).astype(BF16)
        w = ws_ref[g]
        b = bs_ref[g]
        for c in range(tm // CHUNK):
            rows = slice(c * CHUNK, (c + 1) * CHUNK)
            mixed = jnp.dot(w, vn[rows], preferred_element_type=F32) + b
            gate = cg_ref[rows, cols].astype(F32)
            yc = (u_ref[rows, cols].astype(F32) * mixed) * _silu(gate)
            y_scr[rows, D_FOURIER + D_DIFF + g * GROUP:D_FOURIER + D_DIFF + (g + 1) * GROUP] = (
                yc.astype(BF16))
    y = jnp.dot(y_scr[...], wout_ref[...], preferred_element_type=F32)
    ms = jnp.mean(y * y, axis=-1, keepdims=True)
    o_ref[...] = x_ref[...] + (y * lax.rsqrt(ms + EPS)) * pgain_ref[...]


def _outproj(ya, yb, proj, x2, wout_bf, ws_bf, bs3, vgain, pgain):
    s, d = x2.shape
    tm = TM_OUT
    ablk = OFF_A_GATE // D_FOURIER
    ublk, vblk, cblk = OFF_U // D_GMLP, OFF_VSG // D_GMLP, OFF_C_GATE // D_GMLP
    const = lambda shape: pl.BlockSpec(shape, lambda i: tuple(0 for _ in shape))
    return pl.pallas_call(
        _outproj_kernel,
        grid=(s // tm,),
        in_specs=[
            pl.BlockSpec((tm, D_FOURIER), lambda i: (i, 0)),
            pl.BlockSpec((tm, D_FOURIER), lambda i: (i, ablk)),
            pl.BlockSpec((tm, D_DIFF), lambda i: (i, 0)),
            pl.BlockSpec((tm, D_GMLP), lambda i: (i, ublk)),
            pl.BlockSpec((tm, D_GMLP), lambda i: (i, vblk)),
            pl.BlockSpec((tm, D_GMLP), lambda i: (i, cblk)),
            pl.BlockSpec((tm, d), lambda i: (i, 0)),
            const((d, d)),
            const(ws_bf.shape),
            const(bs3.shape),
            const((1, D_GMLP)),
            const((1, d)),
        ],
        out_specs=pl.BlockSpec((tm, d), lambda i: (i, 0)),
        out_shape=jax.ShapeDtypeStruct((s, d), F32),
        scratch_shapes=[pltpu.VMEM((tm, d), BF16)],
        compiler_params=pltpu.CompilerParams(
            dimension_semantics=("arbitrary",), vmem_limit_bytes=VMEM_LIMIT),
        name="outproj",
    )(ya, proj, yb, proj, proj, proj, x2, wout_bf, ws_bf, bs3, vgain, pgain)


def kernel(x, w_in, pre_gain, post_gain, w_fourier, lambda_qk, diff_out_gain, sg_v_gain,
           w_spatial, b_spatial, w_out, rel_bias):
    b, s, d = x.shape
    depth = w_in.shape[0]
    assert b == 1 and w_in.shape[2] == D_IN
    x2 = x.reshape(s, d)

    q_scale = (HALF_DIM ** -0.5) * LOG2E
    colscale = jnp.ones((1, D_IN), F32).at[:, OFF_Q:OFF_Q + D_DIFF].set(q_scale)
    bias = _bias_tiles(rel_bias, T_ATT)
    bias_bounds = jnp.stack([jnp.max(rel_bias, axis=0), jnp.min(rel_bias, axis=0)], axis=1) * LOG2E
    consts = _dft_consts(s // FFT_N2, FFT_N2, GROUP)

    for l in range(depth):
        lambda_init = 0.8 - 0.6 * math.exp(-0.3 * l)
        proj, a_in = _inproj(x2, pre_gain[l].reshape(1, d), w_in[l].astype(BF16), colscale)
        ya = _fourier_branch(a_in, w_fourier[l].astype(BF16), consts)
        qt4 = _head_transposed(proj, OFF_Q, TQ_ATT)
        vt4 = _head_transposed(proj, OFF_V, T_ATT)
        yb = _attention(bias_bounds, lambda_qk[l], qt4, proj, vt4, bias,
                        diff_out_gain[l].reshape(1, D_DIFF), lambda_init)
        x2 = _outproj(ya, yb, proj, x2, w_out[l].astype(BF16), w_spatial[l].astype(BF16),
                      b_spatial[l].reshape(D_GMLP // GROUP, CHUNK, 1),
                      sg_v_gain[l].reshape(1, D_GMLP), post_gain[l].reshape(1, d))
    return x2.reshape(b, s, d)
```

```python
import functools
import math

import numpy as np
import jax
import jax.numpy as jnp
from jax import lax
from jax.experimental import pallas as pl
from jax.experimental.pallas import tpu as pltpu

F32 = jnp.float32
BF16 = jnp.bfloat16

EPS = 1e-6
LOG2E = math.log2(math.e)
NEG_INIT = -1e30
EXP2_SPAN = 120.0

D_FOURIER = 512
D_DIFF = 1024
D_GMLP = 512
GROUP = 128
HEAD_DIM = 128
HALF_DIM = 64
N_HEADS = 8
CHUNK = 128
N_BUCKETS = 32
MAX_DISTANCE = 128
OFF_A_IN, OFF_A_GATE, OFF_Q, OFF_K, OFF_V, OFF_B_GATE, OFF_U, OFF_VSG, OFF_C_GATE = (
    0, 512, 1024, 2048, 3072, 4096, 5120, 5632, 6144)
D_IN = 6656

MXU_COLS = 256
TM_IN = 512
TN_IN = 1664
TM_OUT = 512
T_ATT = 512
TQ_ATT = 1024
N_BIAS_TILES = 4
ZERO_BIAS_TILE = 3
BIAS_COL_SHIFT = 256
FFT_N2 = 128
FFT_ROWS = 8
VMEM_LIMIT = 56 * 1024 * 1024


def _silu(x):
    return x * (1.0 / (1.0 + jnp.exp(-x)))


def _inproj_kernel(x_ref, g_ref, w_ref, cs_ref, proj_ref, ain_ref, a_scr):
    j = pl.program_id(1)

    @pl.when(j == 0)
    def _():
        x = x_ref[...]
        ms = jnp.mean(x * x, axis=-1, keepdims=True)
        a_scr[...] = ((x * lax.rsqrt(ms + EPS)) * g_ref[...]).astype(BF16)

    acc = jnp.dot(a_scr[...], w_ref[...], preferred_element_type=F32)
    proj_ref[...] = (acc * cs_ref[...]).astype(BF16)

    @pl.when(j == 0)
    def _():
        ain_ref[...] = acc[:, :D_FOURIER]


def _inproj(x2, gain, w_bf, colscale):
    s, d = x2.shape
    n = w_bf.shape[1]
    grid = (s // TM_IN, n // TN_IN)
    return pl.pallas_call(
        _inproj_kernel,
        grid=grid,
        in_specs=[
            pl.BlockSpec((TM_IN, d), lambda i, j: (i, 0)),
            pl.BlockSpec((1, d), lambda i, j: (0, 0)),
            pl.BlockSpec((d, TN_IN), lambda i, j: (0, j)),
            pl.BlockSpec((1, TN_IN), lambda i, j: (0, j)),
        ],
        out_specs=[
            pl.BlockSpec((TM_IN, TN_IN), lambda i, j: (i, j)),
            pl.BlockSpec((TM_IN, D_FOURIER), lambda i, j: (i, 0)),
        ],
        out_shape=[
            jax.ShapeDtypeStruct((s, n), BF16),
            jax.ShapeDtypeStruct((s, D_FOURIER), F32),
        ],
        scratch_shapes=[pltpu.VMEM((TM_IN, d), BF16)],
        compiler_params=pltpu.CompilerParams(
            dimension_semantics=("arbitrary", "arbitrary"),
            vmem_limit_bytes=VMEM_LIMIT),
        name="inproj",
    )(x2, gain, w_bf, colscale)


def _dft_consts(n1, n2, c):
    def cs(n):
        idx = np.arange(n)
        ang = 2.0 * np.pi * ((idx[:, None] * idx[None, :]) % n) / n
        return np.cos(ang), np.sin(ang)

    c1, s1 = cs(n1)
    c2, s2 = cs(n2)
    cc, sc = cs(c)
    f1 = np.concatenate([c1, -s1], axis=0)
    f2 = np.block([[c2, s2], [s2, -c2]])
    n = n1 * n2
    k1 = np.arange(n1)[:, None]
    m2 = np.arange(n2)[None, :]
    ang = 2.0 * np.pi * ((k1 * m2) % n) / n
    tw_c = np.cos(ang)[:, :, None]
    tw_s = np.sin(ang)[:, :, None]
    groups = D_FOURIER // c
    bdc = np.kron(np.eye(groups), cc)
    bds = -np.kron(np.eye(groups), sc)
    f32 = lambda a: jnp.asarray(a.astype(np.float32))
    return f32(f1), f32(f2), f32(tw_c), f32(tw_s), f32(bdc), f32(bds)


def _fft1_kernel(f1_ref, x_ref, y_ref):
    f1 = f1_ref[...].astype(BF16)
    for r in range(x_ref.shape[1]):
        xr = x_ref[:, r, :].astype(BF16)
        y_ref[:, r, :] = jnp.dot(f1, xr, preferred_element_type=F32)


def _fft1(f1, x3):
    n1, n2, c = x3.shape
    return pl.pallas_call(
        _fft1_kernel,
        grid=(n2 // FFT_ROWS,),
        in_specs=[pl.BlockSpec((2 * n1, n1), lambda j: (0, 0)),
                  pl.BlockSpec((n1, FFT_ROWS, c), lambda j: (0, j, 0))],
        out_specs=pl.BlockSpec((2 * n1, FFT_ROWS, c), lambda j: (0, j, 0)),
        out_shape=jax.ShapeDtypeStruct((2 * n1, n2, c), F32),
        compiler_params=pltpu.CompilerParams(
            dimension_semantics=("arbitrary",), vmem_limit_bytes=VMEM_LIMIT),
        name="fft_stage1",
    )(f1, x3)


def _fft2_kernel(y_ref, twc_ref, tws_ref, f2_ref, bdc_ref, bds_ref, wf_ref, o_ref, *, scale):
    f2 = f2_ref[...].astype(BF16)
    bdc = bdc_ref[...].astype(BF16)
    bds = bds_ref[...].astype(BF16)
    n2 = y_ref.shape[2]
    for r in range(y_ref.shape[1]):
        yr = y_ref[0, r]
        yi = y_ref[1, r]
        tc = twc_ref[r]
        ts = tws_ref[r]
        zr = yr * tc + yi * ts
        zi = yi * tc - yr * ts
        z = jnp.concatenate([zr, zi], axis=0).astype(BF16)
        pq = jnp.dot(f2, z, preferred_element_type=F32)
        p = pq[:n2].astype(BF16)
        q = pq[n2:].astype(BF16)
        f = (jnp.dot(p, bdc, preferred_element_type=F32)
             + jnp.dot(q, bds, preferred_element_type=F32)) * scale
        o_ref[:, r, :] = jnp.dot(f.astype(BF16), wf_ref[...], preferred_element_type=F32)


def _fft2(y4, twc, tws, f2, bdc, bds, wf_bf, scale):
    _, n1, n2, c = y4.shape
    const = lambda shape: pl.BlockSpec(shape, lambda k: tuple(0 for _ in shape))
    return pl.pallas_call(
        functools.partial(_fft2_kernel, scale=scale),
        grid=(n1 // FFT_ROWS,),
        in_specs=[
            pl.BlockSpec((2, FFT_ROWS, n2, c), lambda k: (0, k, 0, 0)),
            pl.BlockSpec((FFT_ROWS, n2, 1), lambda k: (k, 0, 0)),
            pl.BlockSpec((FFT_ROWS, n2, 1), lambda k: (k, 0, 0)),
            const((2 * n2, 2 * n2)),
            const((c, c)),
            const((c, c)),
            const((c, c)),
        ],
        out_specs=pl.BlockSpec((n2, FFT_ROWS, c), lambda k: (0, k, 0)),
        out_shape=jax.ShapeDtypeStruct((n2, n1, c), F32),
        compiler_params=pltpu.CompilerParams(
            dimension_semantics=("arbitrary",), vmem_limit_bytes=VMEM_LIMIT),
        name="fft_stage2",
    )(y4, twc, tws, f2, bdc, bds, wf_bf)


def _fourier_branch(a_in, wf_bf, consts):
    s, c = a_in.shape
    n2 = FFT_N2
    n1 = s // n2
    f1, f2, twc, tws, bdc, bds = consts
    y = _fft1(f1, a_in.reshape(n1, n2, c))
    scale = 1.0 / math.sqrt(s * GROUP)
    out = _fft2(y.reshape(2, n1, n2, c), twc, tws, f2, bdc, bds, wf_bf, scale)
    return out.reshape(s, c)


def _t5_bucket(rel):
    nb = N_BUCKETS // 2
    ret = (rel > 0).astype(jnp.int32) * nb
    n = jnp.abs(rel)
    max_exact = nb // 2
    nf = jnp.maximum(n, 1).astype(jnp.float32)
    large = max_exact + (jnp.log(nf / max_exact) / math.log(MAX_DISTANCE / max_exact)
                         * (nb - max_exact)).astype(jnp.int32)
    large = jnp.minimum(large, nb - 1)
    return ret + jnp.where(n < max_exact, n, large)


def _bias_kernel(rb_ref, bk_ref, o_ref):
    h = pl.program_id(0)
    bk = bk_ref[0]
    acc = jnp.zeros(bk.shape, F32)
    for b in range(N_BUCKETS):
        acc = jnp.where(bk == b, rb_ref[b, h], acc)
    o_ref[0, 0] = acc * LOG2E


def _bias_tiles(rel_bias, t):
    kk = jnp.arange(t, dtype=jnp.int32)[:, None]
    rr = ((jnp.arange(t, dtype=jnp.int32) - BIAS_COL_SHIFT) % t)[None, :]
    rel = jnp.stack([d * t + kk - rr for d in (-1, 0, 1)])
    bucket = jnp.concatenate([_t5_bucket(rel), jnp.full((1, t, t), -1, jnp.int32)])
    return pl.pallas_call(
        _bias_kernel,
        grid=(N_HEADS, N_BIAS_TILES),
        in_specs=[pl.BlockSpec(memory_space=pltpu.SMEM),
                  pl.BlockSpec((1, t, t), lambda h, d: (d, 0, 0))],
        out_specs=pl.BlockSpec((1, 1, t, t), lambda h, d: (h, d, 0, 0)),
        out_shape=jax.ShapeDtypeStruct((N_HEADS, N_BIAS_TILES, t, t), F32),
        compiler_params=pltpu.CompilerParams(
            dimension_semantics=("arbitrary", "arbitrary"), vmem_limit_bytes=VMEM_LIMIT),
        name="bias_tiles",
    )(rel_bias, bucket)


def _attn_kernel(bnd_ref, lq_ref, qt_ref, k_ref, vt_ref, bias_ref, gate_ref, gain_ref, o_ref,
                 q2_scr, m_scr, l_scr, acc_scr, kabs_scr, mreg_scr, pa_scr, pb_scr,
                 *, lambda_init, n_tiles):
    tq = qt_ref.shape[-1]
    t = vt_ref.shape[-1]
    nqb = tq // t
    h = pl.program_id(0)
    i = pl.program_id(1)

    @pl.when(i == 0)
    def _():
        ka = jnp.max(jnp.abs(k_ref[...].astype(F32)), axis=0, keepdims=True)
        kabs_scr[...] = jnp.broadcast_to(ka, (8, HEAD_DIM)).T

    qt = qt_ref[0, 0]
    row = lax.broadcasted_iota(jnp.int32, qt.shape, 0)
    zero = jnp.zeros_like(qt)
    q2_scr[:, :tq] = jnp.where(row < HALF_DIM, qt, zero)
    q2_scr[:, tq:] = jnp.where(row >= HALF_DIM, qt, zero)
    l_scr[...] = jnp.zeros(l_scr.shape, F32)
    acc_scr[...] = jnp.zeros(acc_scr.shape, F32)

    cl_col = (t - 1 + BIAS_COL_SHIFT) % t
    cr_col = BIAS_COL_SHIFT % t
    c_left = bias_ref[0, 0, 0:1, cl_col:cl_col + 1]
    c_right = bias_ref[0, 2, t - 1:t, cr_col:cr_col + 1]

    wq = jnp.abs(qt.astype(F32)) * kabs_scr[:, 0:1]
    hb = jnp.concatenate([jnp.sum(wq[:HALF_DIM], axis=0, keepdims=True),
                          jnp.sum(wq[HALF_DIM:], axis=0, keepdims=True)], axis=1)
    b_max = bnd_ref[h, 0]
    b_min = bnd_ref[h, 1]
    fixed_ok = 2.0 * jnp.max(hb) + (b_max - b_min) <= EXP2_SPAN

    def block_relation(j, col0):
        e = j - (nqb * i + (col0 % tq) // t)
        tile_idx = jnp.where(jnp.abs(e) <= 1, e + 1, ZERO_BIAS_TILE)
        region = jnp.where(e < -1, 0, jnp.where(e > 1, 2, 1))
        return tile_idx, region

    def step(j_p, p_out, j_v, p_in, with_bias):
        if j_p is not None:
            kt = k_ref[pl.ds(pl.multiple_of(j_p * t, t), t), :]
        if j_v is not None:
            vt = vt_ref[0, j_v]
        for c0 in range(0, 2 * tq, MXU_COLS):
            cs = slice(c0, c0 + MXU_COLS)
            if j_p is not None:
                tile_idx, region = block_relation(j_p, c0)
                s = jnp.dot(kt, q2_scr[:, cs], preferred_element_type=F32)
                if with_bias:
                    b0 = (c0 + BIAS_COL_SHIFT) % t
                    s = s + bias_ref[0, tile_idx, :, b0:b0 + MXU_COLS]
                p = jnp.exp2(s - mreg_scr[region, :, cs])
                l_scr[:, cs] += jnp.sum(p, axis=0, keepdims=True)
                p_out[:, cs] = p.astype(BF16)
            if j_v is not None:
                acc_scr[:, cs] += jnp.dot(vt, p_in[:, cs], preferred_element_type=F32)

    def near_band(j):
        return jnp.logical_and(j >= nqb * i - 1, j <= nqb * i + nqb)

    def step_pair(j, last):
        def run(with_bias):
            step(j + 1, pb_scr, j, pa_scr, with_bias)
            step(None if last else j + 2, pa_scr, j + 1, pb_scr, with_bias)

        near = near_band(j + 1) if last else jnp.logical_or(near_band(j + 1), near_band(j + 2))
        lax.cond(near, lambda: run(True), lambda: run(False))

    @pl.when(fixed_ok)
    def _():
        m0 = hb + b_max
        mreg_scr[0] = m0 - c_left
        mreg_scr[1] = m0
        mreg_scr[2] = m0 - c_right

        def pair_body(jj, carry):
            step_pair(2 * jj, last=False)
            return carry

        step(0, pa_scr, None, None, True)
        lax.fori_loop(0, n_tiles // 2 - 1, pair_body, 0)
        step_pair(n_tiles - 2, last=True)

    def online_step(j):
        kt = k_ref[pl.ds(pl.multiple_of(j * t, t), t), :]
        vt = vt_ref[0, j]
        for c0 in range(0, 2 * tq, t):
            cs = slice(c0, c0 + t)
            tile_idx, region = block_relation(j, c0)
            c_far = jnp.where(region == 0, c_left, jnp.where(region == 2, c_right, 0.0))
            b = bias_ref[0, tile_idx]
            b = jnp.concatenate([b[:, BIAS_COL_SHIFT:], b[:, :BIAS_COL_SHIFT]], axis=1)
            s = jnp.dot(kt, q2_scr[:, cs], preferred_element_type=F32) + (b + c_far)
            m_old = m_scr[:, cs]
            m_new = jnp.maximum(m_old, jnp.max(s, axis=0, keepdims=True))
            p = jnp.exp2(s - m_new)
            alpha = jnp.exp2(m_old - m_new)
            l_scr[:, cs] = alpha * l_scr[:, cs] + jnp.sum(p, axis=0, keepdims=True)
            pv = jnp.dot(vt, p.astype(BF16), preferred_element_type=F32)
            acc_scr[:, cs] = alpha * acc_scr[:, cs] + pv
            m_scr[:, cs] = m_new

    @pl.when(jnp.logical_not(fixed_ok))
    def _():
        def body(j, carry):
            online_step(j)
            return carry

        m_scr[...] = jnp.full(m_scr.shape, NEG_INIT, F32)
        lax.fori_loop(0, n_tiles, body, 0)

    lq = lq_ref[...]
    lam = (jnp.exp(jnp.sum(lq[0:1] * lq[1:2], axis=-1, keepdims=True))
           - jnp.exp(jnp.sum(lq[2:3] * lq[3:4], axis=-1, keepdims=True)) + lambda_init)
    accn = acc_scr[...] * (1.0 / l_scr[...])
    o_t = accn[:, :tq] - lam * accn[:, tq:]
    o = o_t.T
    ms = jnp.mean(o * o, axis=-1, keepdims=True)
    y = ((o * lax.rsqrt(ms + EPS)) * gain_ref[...]) * (1.0 - lambda_init)
    o_ref[...] = (y * _silu(gate_ref[...].astype(F32))).astype(BF16)


def _attention(bnd, lq, qt4, proj, vt4, bias, gain, lambda_init):
    s = proj.shape[0]
    tq = qt4.shape[-1]
    t = vt4.shape[-1]
    n_tiles = s // t
    kblk = OFF_K // HEAD_DIM
    gblk = OFF_B_GATE // HEAD_DIM
    kern = functools.partial(_attn_kernel, lambda_init=lambda_init, n_tiles=n_tiles)
    return pl.pallas_call(
        kern,
        grid=(N_HEADS, s // tq),
        in_specs=[
            pl.BlockSpec(memory_space=pltpu.SMEM),
            pl.BlockSpec((4, HALF_DIM), lambda h, i: (0, 0)),
            pl.BlockSpec((1, 1, HEAD_DIM, tq), lambda h, i: (h, i, 0, 0)),
            pl.BlockSpec((s, HEAD_DIM), lambda h, i: (0, kblk + h)),
            pl.BlockSpec((1, n_tiles, HEAD_DIM, t), lambda h, i: (h, 0, 0, 0)),
            pl.BlockSpec((1, N_BIAS_TILES, t, t), lambda h, i: (h, 0, 0, 0)),
            pl.BlockSpec((tq, HEAD_DIM), lambda h, i: (i, gblk + h)),
            pl.BlockSpec((1, HEAD_DIM), lambda h, i: (0, h)),
        ],
        out_specs=pl.BlockSpec((tq, HEAD_DIM), lambda h, i: (i, h)),
        out_shape=jax.ShapeDtypeStruct((s, D_DIFF), BF16),
        scratch_shapes=[
            pltpu.VMEM((HEAD_DIM, 2 * tq), BF16),
            pltpu.VMEM((1, 2 * tq), F32),
            pltpu.VMEM((1, 2 * tq), F32),
            pltpu.VMEM((HEAD_DIM, 2 * tq), F32),
            pltpu.VMEM((HEAD_DIM, 8), F32),
            pltpu.VMEM((3, 1, 2 * tq), F32),
            pltpu.VMEM((t, 2 * tq), BF16),
            pltpu.VMEM((t, 2 * tq), BF16),
        ],
        compiler_params=pltpu.CompilerParams(
            dimension_semantics=("arbitrary", "arbitrary"), vmem_limit_bytes=VMEM_LIMIT),
        name="diff_attention",
    )(bnd, lq, qt4, proj, vt4, bias, proj, gain)


def _head_transposed(proj, off, t):
    s = proj.shape[0]
    a = proj[:, off:off + D_DIFF].reshape(s // t, t, N_HEADS, HEAD_DIM)
    return a.transpose(2, 0, 3, 1)


def _outproj_kernel(ya_ref, ag_ref, yb_ref, u_ref, vsg_ref, cg_ref, x_ref, wout_ref, ws_ref,
                    bs_ref, vgain_ref, pgain_ref, o_ref, y_scr):
    tm = x_ref.shape[0]
    y_scr[:, 0:D_FOURIER] = (ya_ref[...] * _silu(ag_ref[...].astype(F32))).astype(BF16)
    y_scr[:, D_FOURIER:D_FOURIER + D_DIFF] = yb_ref[...]
    for g in range(D_GMLP // GROUP):
        cols = slice(g * GROUP, (g + 1) * GROUP)
        v = vsg_ref[:, cols].astype(F32)
        ms = jnp.mean(v * v, axis=-1, keepdims=True)
        vn = ((v * lax.rsqrt(ms + EPS)) * vgain_ref[:, cols]).astype(BF16)
        w = ws_ref[g]
        b = bs_ref[g]
        for c in range(tm // CHUNK):
            rows = slice(c * CHUNK, (c + 1) * CHUNK)
            mixed = jnp.dot(w, vn[rows], preferred_element_type=F32) + b
            gate = cg_ref[rows, cols].astype(F32)
            yc = (u_ref[rows, cols].astype(F32) * mixed) * _silu(gate)
            y_scr[rows, D_FOURIER + D_DIFF + g * GROUP:D_FOURIER + D_DIFF + (g + 1) * GROUP] = (
                yc.astype(BF16))
    y = jnp.dot(y_scr[...], wout_ref[...], preferred_element_type=F32)
    ms = jnp.mean(y * y, axis=-1, keepdims=True)
    o_ref[...] = x_ref[...] + (y * lax.rsqrt(ms + EPS)) * pgain_ref[...]


def _outproj(ya, yb, proj, x2, wout_bf, ws_bf, bs3, vgain, pgain):
    s, d = x2.shape
    tm = TM_OUT
    ablk = OFF_A_GATE // D_FOURIER
    ublk, vblk, cblk = OFF_U // D_GMLP, OFF_VSG // D_GMLP, OFF_C_GATE // D_GMLP
    const = lambda shape: pl.BlockSpec(shape, lambda i: tuple(0 for _ in shape))
    return pl.pallas_call(
        _outproj_kernel,
        grid=(s // tm,),
        in_specs=[
            pl.BlockSpec((tm, D_FOURIER), lambda i: (i, 0)),
            pl.BlockSpec((tm, D_FOURIER), lambda i: (i, ablk)),
            pl.BlockSpec((tm, D_DIFF), lambda i: (i, 0)),
            pl.BlockSpec((tm, D_GMLP), lambda i: (i, ublk)),
            pl.BlockSpec((tm, D_GMLP), lambda i: (i, vblk)),
            pl.BlockSpec((tm, D_GMLP), lambda i: (i, cblk)),
            pl.BlockSpec((tm, d), lambda i: (i, 0)),
            const((d, d)),
            const(ws_bf.shape),
            const(bs3.shape),
            const((1, D_GMLP)),
            const((1, d)),
        ],
        out_specs=pl.BlockSpec((tm, d), lambda i: (i, 0)),
        out_shape=jax.ShapeDtypeStruct((s, d), F32),
        scratch_shapes=[pltpu.VMEM((tm, d), BF16)],
        compiler_params=pltpu.CompilerParams(
            dimension_semantics=("arbitrary",), vmem_limit_bytes=VMEM_LIMIT),
        name="outproj",
    )(ya, proj, yb, proj, proj, proj, x2, wout_bf, ws_bf, bs3, vgain, pgain)


def kernel(x, w_in, pre_gain, post_gain, w_fourier, lambda_qk, diff_out_gain, sg_v_gain,
           w_spatial, b_spatial, w_out, rel_bias):
    b, s, d = x.shape
    depth = w_in.shape[0]
    assert b == 1 and w_in.shape[2] == D_IN
    x2 = x.reshape(s, d)

    q_scale = (HALF_DIM ** -0.5) * LOG2E
    colscale = jnp.ones((1, D_IN), F32).at[:, OFF_Q:OFF_Q + D_DIFF].set(q_scale)
    bias = _bias_tiles(rel_bias, T_ATT)
    bias_bounds = jnp.stack([jnp.max(rel_bias, axis=0), jnp.min(rel_bias, axis=0)], axis=1) * LOG2E
    consts = _dft_consts(s // FFT_N2, FFT_N2, GROUP)

    for l in range(depth):
        lambda_init = 0.8 - 0.6 * math.exp(-0.3 * l)
        proj, a_in = _inproj(x2, pre_gain[l].reshape(1, d), w_in[l].astype(BF16), colscale)
        ya = _fourier_branch(a_in, w_fourier[l].astype(BF16), consts)
        qt4 = _head_transposed(proj, OFF_Q, TQ_ATT)
        vt4 = _head_transposed(proj, OFF_V, T_ATT)
        yb = _attention(bias_bounds, lambda_qk[l], qt4, proj, vt4, bias,
                        diff_out_gain[l].reshape(1, D_DIFF), lambda_init)
        x2 = _outproj(ya, yb, proj, x2, w_out[l].astype(BF16), w_spatial[l].astype(BF16),
                      b_spatial[l].reshape(D_GMLP // GROUP, CHUNK, 1),
                      sg_v_gain[l].reshape(1, D_GMLP), post_gain[l].reshape(1, d))
    return x2.reshape(b, s, d)
```

```python
import functools
import math

import numpy as np
import jax
import jax.numpy as jnp
from jax import lax
from jax.experimental import pallas as pl
from jax.experimental.pallas import tpu as pltpu

F32 = jnp.float32
BF16 = jnp.bfloat16

EPS = 1e-6
LOG2E = math.log2(math.e)
NEG_INIT = -1e30
EXP2_SPAN = 120.0

D_FOURIER = 512
D_DIFF = 1024
D_GMLP = 512
GROUP = 128
HEAD_DIM = 128
HALF_DIM = 64
N_HEADS = 8
CHUNK = 128
N_BUCKETS = 32
MAX_DISTANCE = 128
OFF_A_IN, OFF_A_GATE, OFF_Q, OFF_K, OFF_V, OFF_B_GATE, OFF_U, OFF_VSG, OFF_C_GATE = (
    0, 512, 1024, 2048, 3072, 4096, 5120, 5632, 6144)
D_IN = 6656

MXU_COLS = 256
TM_IN = 512
TN_IN = 1664
TM_OUT = 512
T_ATT = 512
TQ_ATT = 2048
N_BIAS_TILES = 4
ZERO_BIAS_TILE = 3
BIAS_COL_SHIFT = 256
FFT_N2 = 128
FFT_ROWS = 8
VMEM_LIMIT = 56 * 1024 * 1024
assert TM_IN == T_ATT


def _silu(x):
    return x * (1.0 / (1.0 + jnp.exp(-x)))


def _inproj_kernel(x_ref, g_ref, w_ref, cs_ref, proj_ref, ain_ref, qt_ref, vt_ref, a_scr):
    j = pl.program_id(1)

    @pl.when(j == 0)
    def _():
        x = x_ref[...]
        ms = jnp.mean(x * x, axis=-1, keepdims=True)
        a_scr[...] = ((x * lax.rsqrt(ms + EPS)) * g_ref[...]).astype(BF16)

    acc = jnp.dot(a_scr[...], w_ref[...], preferred_element_type=F32)
    scaled = acc * cs_ref[...]
    proj_ref[...] = scaled.astype(BF16)

    @pl.when(j == 0)
    def _():
        ain_ref[...] = acc[:, :D_FOURIER]

    for jj in range(D_IN // TN_IN):
        heads = [(dst, (c - off) // HEAD_DIM, c - jj * TN_IN)
                 for dst, off in ((qt_ref, OFF_Q), (vt_ref, OFF_V))
                 for c in range(off, off + D_DIFF, HEAD_DIM)
                 if jj * TN_IN <= c < (jj + 1) * TN_IN]
        if heads:
            @pl.when(j == jj)
            def _(heads=heads):
                for dst, h, c0 in heads:
                    dst[h, 0] = scaled[:, c0:c0 + HEAD_DIM].T.astype(BF16)


def _inproj(x2, gain, w_bf, layer, colscale):
    s, d = x2.shape
    n = w_bf.shape[2]
    grid = (s // TM_IN, n // TN_IN)
    head_tiles = pl.BlockSpec((N_HEADS, 1, HEAD_DIM, TM_IN), lambda i, j: (0, i, 0, 0))
    return pl.pallas_call(
        _inproj_kernel,
        grid=grid,
        in_specs=[
            pl.BlockSpec((TM_IN, d), lambda i, j: (i, 0)),
            pl.BlockSpec((1, d), lambda i, j: (0, 0)),
            pl.BlockSpec((None, d, TN_IN), lambda i, j: (layer, 0, j)),
            pl.BlockSpec((1, TN_IN), lambda i, j: (0, j)),
        ],
        out_specs=[
            pl.BlockSpec((TM_IN, TN_IN), lambda i, j: (i, j)),
            pl.BlockSpec((TM_IN, D_FOURIER), lambda i, j: (i, 0)),
            head_tiles,
            head_tiles,
        ],
        out_shape=[
            jax.ShapeDtypeStruct((s, n), BF16),
            jax.ShapeDtypeStruct((s, D_FOURIER), F32),
            jax.ShapeDtypeStruct((N_HEADS, s // TM_IN, HEAD_DIM, TM_IN), BF16),
            jax.ShapeDtypeStruct((N_HEADS, s // TM_IN, HEAD_DIM, TM_IN), BF16),
        ],
        scratch_shapes=[pltpu.VMEM((TM_IN, d), BF16)],
        compiler_params=pltpu.CompilerParams(
            dimension_semantics=("arbitrary", "arbitrary"),
            vmem_limit_bytes=VMEM_LIMIT),
        name="inproj",
    )(x2, gain, w_bf, colscale)


def _dft_consts(n1, n2, c):
    def cs(n):
        idx = np.arange(n)
        ang = 2.0 * np.pi * ((idx[:, None] * idx[None, :]) % n) / n
        return np.cos(ang), np.sin(ang)

    c1, s1 = cs(n1)
    c2, s2 = cs(n2)
    cc, sc = cs(c)
    f1 = np.concatenate([c1, -s1], axis=0)
    f2 = np.block([[c2, s2], [s2, -c2]])
    n = n1 * n2
    k1 = np.arange(n1)[:, None]
    m2 = np.arange(n2)[None, :]
    ang = 2.0 * np.pi * ((k1 * m2) % n) / n
    tw_c = np.cos(ang)[:, :, None]
    tw_s = np.sin(ang)[:, :, None]
    groups = D_FOURIER // c
    bdc = np.kron(np.eye(groups), cc)
    bds = -np.kron(np.eye(groups), sc)
    f32 = lambda a: jnp.asarray(a.astype(np.float32))
    return f32(f1), f32(f2), f32(tw_c), f32(tw_s), f32(bdc), f32(bds)


def _fft1_kernel(f1_ref, x_ref, y_ref):
    f1 = f1_ref[...].astype(BF16)
    for r in range(x_ref.shape[1]):
        xr = x_ref[:, r, :].astype(BF16)
        y_ref[:, r, :] = jnp.dot(f1, xr, preferred_element_type=F32)


def _fft1(f1, x3):
    n1, n2, c = x3.shape
    return pl.pallas_call(
        _fft1_kernel,
        grid=(n2 // FFT_ROWS,),
        in_specs=[pl.BlockSpec((2 * n1, n1), lambda j: (0, 0)),
                  pl.BlockSpec((n1, FFT_ROWS, c), lambda j: (0, j, 0))],
        out_specs=pl.BlockSpec((2 * n1, FFT_ROWS, c), lambda j: (0, j, 0)),
        out_shape=jax.ShapeDtypeStruct((2 * n1, n2, c), F32),
        compiler_params=pltpu.CompilerParams(
            dimension_semantics=("arbitrary",), vmem_limit_bytes=VMEM_LIMIT),
        name="fft_stage1",
    )(f1, x3)


def _fft2_kernel(y_ref, twc_ref, tws_ref, f2_ref, bdc_ref, bds_ref, wf_ref, o_ref, *, scale):
    f2 = f2_ref[...].astype(BF16)
    bdc = bdc_ref[...].astype(BF16)
    bds = bds_ref[...].astype(BF16)
    n2 = y_ref.shape[2]
    for r in range(y_ref.shape[1]):
        yr = y_ref[0, r]
        yi = y_ref[1, r]
        tc = twc_ref[r]
        ts = tws_ref[r]
        zr = yr * tc + yi * ts
        zi = yi * tc - yr * ts
        z = jnp.concatenate([zr, zi], axis=0).astype(BF16)
        pq = jnp.dot(f2, z, preferred_element_type=F32)
        p = pq[:n2].astype(BF16)
        q = pq[n2:].astype(BF16)
        f = (jnp.dot(p, bdc, preferred_element_type=F32)
             + jnp.dot(q, bds, preferred_element_type=F32)) * scale
        o_ref[:, r, :] = jnp.dot(f.astype(BF16), wf_ref[...], preferred_element_type=F32)


def _fft2(y4, twc, tws, f2, bdc, bds, wf_bf, layer, scale):
    _, n1, n2, c = y4.shape
    const = lambda shape: pl.BlockSpec(shape, lambda k: tuple(0 for _ in shape))
    return pl.pallas_call(
        functools.partial(_fft2_kernel, scale=scale),
        grid=(n1 // FFT_ROWS,),
        in_specs=[
            pl.BlockSpec((2, FFT_ROWS, n2, c), lambda k: (0, k, 0, 0)),
            pl.BlockSpec((FFT_ROWS, n2, 1), lambda k: (k, 0, 0)),
            pl.BlockSpec((FFT_ROWS, n2, 1), lambda k: (k, 0, 0)),
            const((2 * n2, 2 * n2)),
            const((c, c)),
            const((c, c)),
            pl.BlockSpec((None, c, c), lambda k: (layer, 0, 0)),
        ],
        out_specs=pl.BlockSpec((n2, FFT_ROWS, c), lambda k: (0, k, 0)),
        out_shape=jax.ShapeDtypeStruct((n2, n1, c), F32),
        compiler_params=pltpu.CompilerParams(
            dimension_semantics=("arbitrary",), vmem_limit_bytes=VMEM_LIMIT),
        name="fft_stage2",
    )(y4, twc, tws, f2, bdc, bds, wf_bf)


def _fourier_branch(a_in, wf_bf, layer, consts):
    s, c = a_in.shape
    n2 = FFT_N2
    n1 = s // n2
    f1, f2, twc, tws, bdc, bds = consts
    y = _fft1(f1, a_in.reshape(n1, n2, c))
    scale = 1.0 / math.sqrt(s * GROUP)
    out = _fft2(y.reshape(2, n1, n2, c), twc, tws, f2, bdc, bds, wf_bf, layer,
                scale)
    return out.reshape(s, c)


def _t5_bucket(rel):
    nb = N_BUCKETS // 2
    ret = (rel > 0).astype(jnp.int32) * nb
    n = jnp.abs(rel)
    max_exact = nb // 2
    nf = jnp.maximum(n, 1).astype(jnp.float32)
    large = max_exact + (jnp.log(nf / max_exact) / math.log(MAX_DISTANCE / max_exact)
                         * (nb - max_exact)).astype(jnp.int32)
    large = jnp.minimum(large, nb - 1)
    return ret + jnp.where(n < max_exact, n, large)


def _bias_kernel(rb_ref, bk_ref, o_ref):
    h = pl.program_id(0)
    bk = bk_ref[0]
    acc = jnp.zeros(bk.shape, F32)
    for b in range(N_BUCKETS):
        acc = jnp.where(bk == b, rb_ref[b, h], acc)
    o_ref[0, 0] = acc * LOG2E


def _bias_tiles(rel_bias, t):
    kk = jnp.arange(t, dtype=jnp.int32)[:, None]
    rr = ((jnp.arange(t, dtype=jnp.int32) - BIAS_COL_SHIFT) % t)[None, :]
    rel = jnp.stack([d * t + kk - rr for d in (-1, 0, 1)])
    bucket = jnp.concatenate([_t5_bucket(rel), jnp.full((1, t, t), -1, jnp.int32)])
    return pl.pallas_call(
        _bias_kernel,
        grid=(N_HEADS, N_BIAS_TILES),
        in_specs=[pl.BlockSpec(memory_space=pltpu.SMEM),
                  pl.BlockSpec((1, t, t), lambda h, d: (d, 0, 0))],
        out_specs=pl.BlockSpec((1, 1, t, t), lambda h, d: (h, d, 0, 0)),
        out_shape=jax.ShapeDtypeStruct((N_HEADS, N_BIAS_TILES, t, t), F32),
        compiler_params=pltpu.CompilerParams(
            dimension_semantics=("arbitrary", "arbitrary"), vmem_limit_bytes=VMEM_LIMIT),
        name="bias_tiles",
    )(rel_bias, bucket)


def _attn_kernel(bnd_ref, lq_ref, qt_ref, k_ref, vt_ref, bias_ref, gate_ref, gain_ref, o_ref,
                 q2_scr, m_scr, l_scr, acc_scr, kabs_scr, mreg_scr, pa_scr, pb_scr,
                 *, lambda_init, n_tiles):
    tq = qt_ref.shape[1] * qt_ref.shape[3]
    t = vt_ref.shape[-1]
    nqb = tq // t
    h = pl.program_id(0)
    i = pl.program_id(1)

    @pl.when(i == 0)
    def _():
        ka = jnp.max(jnp.abs(k_ref[...].astype(F32)), axis=0, keepdims=True)
        kabs_scr[...] = jnp.broadcast_to(ka, (8, HEAD_DIM)).T

    qt = jnp.concatenate([qt_ref[0, r] for r in range(qt_ref.shape[1])], axis=1)
    row = lax.broadcasted_iota(jnp.int32, qt.shape, 0)
    zero = jnp.zeros_like(qt)
    q2_scr[:, :tq] = jnp.where(row < HALF_DIM, qt, zero)
    q2_scr[:, tq:] = jnp.where(row >= HALF_DIM, qt, zero)
    l_scr[...] = jnp.zeros(l_scr.shape, F32)
    acc_scr[...] = jnp.zeros(acc_scr.shape, F32)

    cl_col = (t - 1 + BIAS_COL_SHIFT) % t
    cr_col = BIAS_COL_SHIFT % t
    c_left = bias_ref[0, 0, 0:1, cl_col:cl_col + 1]
    c_right = bias_ref[0, 2, t - 1:t, cr_col:cr_col + 1]

    wq = jnp.abs(qt.astype(F32)) * kabs_scr[:, 0:1]
    hb = jnp.concatenate([jnp.sum(wq[:HALF_DIM], axis=0, keepdims=True),
                          jnp.sum(wq[HALF_DIM:], axis=0, keepdims=True)], axis=1)
    b_max = bnd_ref[h, 0]
    b_min = bnd_ref[h, 1]
    fixed_ok = 2.0 * jnp.max(hb) + (b_max - b_min) <= EXP2_SPAN

    def block_relation(j, col0):
        e = j - (nqb * i + (col0 % tq) // t)
        tile_idx = jnp.where(jnp.abs(e) <= 1, e + 1, ZERO_BIAS_TILE)
        region = jnp.where(e < -1, 0, jnp.where(e > 1, 2, 1))
        return tile_idx, region

    def step(j_p, p_out, j_v, p_in, with_bias):
        if j_p is not None:
            kt = k_ref[pl.ds(pl.multiple_of(j_p * t, t), t), :]
        if j_v is not None:
            vt = vt_ref[0, j_v]
        for c0 in range(0, 2 * tq, MXU_COLS):
            cs = slice(c0, c0 + MXU_COLS)
            if j_p is not None:
                tile_idx, region = block_relation(j_p, c0)
                s = jnp.dot(kt, q2_scr[:, cs], preferred_element_type=F32)
                if with_bias:
                    b0 = (c0 + BIAS_COL_SHIFT) % t
                    s = s + bias_ref[0, tile_idx, :, b0:b0 + MXU_COLS]
                p = jnp.exp2(s - mreg_scr[region, :, cs])
                l_scr[:, cs] += jnp.sum(p, axis=0, keepdims=True)
                p_out[:, cs] = p.astype(BF16)
            if j_v is not None:
                acc_scr[:, cs] += jnp.dot(vt, p_in[:, cs], preferred_element_type=F32)

    def near_band(j):
        return jnp.logical_and(j >= nqb * i - 1, j <= nqb * i + nqb)

    def step_pair(j, last):
        def run(with_bias):
            step(j + 1, pb_scr, j, pa_scr, with_bias)
            step(None if last else j + 2, pa_scr, j + 1, pb_scr, with_bias)

        near = near_band(j + 1) if last else jnp.logical_or(near_band(j + 1), near_band(j + 2))
        lax.cond(near, lambda: run(True), lambda: run(False))

    @pl.when(fixed_ok)
    def _():
        m0 = hb + b_max
        mreg_scr[0] = m0 - c_left
        mreg_scr[1] = m0
        mreg_scr[2] = m0 - c_right

        def pair_body(jj, carry):
            step_pair(2 * jj, last=False)
            return carry

        step(0, pa_scr, None, None, True)
        lax.fori_loop(0, n_tiles // 2 - 1, pair_body, 0)
        step_pair(n_tiles - 2, last=True)

    def online_step(j):
        kt = k_ref[pl.ds(pl.multiple_of(j * t, t), t), :]
        vt = vt_ref[0, j]
        for c0 in range(0, 2 * tq, t):
            cs = slice(c0, c0 + t)
            tile_idx, region = block_relation(j, c0)
            c_far = jnp.where(region == 0, c_left, jnp.where(region == 2, c_right, 0.0))
            b = bias_ref[0, tile_idx]
            b = jnp.concatenate([b[:, BIAS_COL_SHIFT:], b[:, :BIAS_COL_SHIFT]], axis=1)
            s = jnp.dot(kt, q2_scr[:, cs], preferred_element_type=F32) + (b + c_far)
            m_old = m_scr[:, cs]
            m_new = jnp.maximum(m_old, jnp.max(s, axis=0, keepdims=True))
            p = jnp.exp2(s - m_new)
            alpha = jnp.exp2(m_old - m_new)
            l_scr[:, cs] = alpha * l_scr[:, cs] + jnp.sum(p, axis=0, keepdims=True)
            pv = jnp.dot(vt, p.astype(BF16), preferred_element_type=F32)
            acc_scr[:, cs] = alpha * acc_scr[:, cs] + pv
            m_scr[:, cs] = m_new

    @pl.when(jnp.logical_not(fixed_ok))
    def _():
        def body(j, carry):
            online_step(j)
            return carry

        m_scr[...] = jnp.full(m_scr.shape, NEG_INIT, F32)
        lax.fori_loop(0, n_tiles, body, 0)

    lq = lq_ref[...]
    lam = (jnp.exp(jnp.sum(lq[0:1] * lq[1:2], axis=-1, keepdims=True))
           - jnp.exp(jnp.sum(lq[2:3] * lq[3:4], axis=-1, keepdims=True)) + lambda_init)
    accn = acc_scr[...] * (1.0 / l_scr[...])
    o_t = accn[:, :tq] - lam * accn[:, tq:]
    o = o_t.T
    ms = jnp.mean(o * o, axis=-1, keepdims=True)
    y = ((o * lax.rsqrt(ms + EPS)) * gain_ref[...]) * (1.0 - lambda_init)
    o_ref[...] = (y * _silu(gate_ref[...].astype(F32))).astype(BF16)


def _attention(bnd, lq, qt4, proj, vt4, bias, gain, lambda_init):
    s = proj.shape[0]
    tq = TQ_ATT
    t = vt4.shape[-1]
    assert t == bias.shape[-1] and tq % t == 0
    n_tiles = s // t
    kblk = OFF_K // HEAD_DIM
    gblk = OFF_B_GATE // HEAD_DIM
    kern = functools.partial(_attn_kernel, lambda_init=lambda_init, n_tiles=n_tiles)
    return pl.pallas_call(
        kern,
        grid=(N_HEADS, s // tq),
        in_specs=[
            pl.BlockSpec(memory_space=pltpu.SMEM),
            pl.BlockSpec((4, HALF_DIM), lambda h, i: (0, 0)),
            pl.BlockSpec((1, tq // t, HEAD_DIM, t), lambda h, i: (h, i, 0, 0)),
            pl.BlockSpec((s, HEAD_DIM), lambda h, i: (0, kblk + h)),
            pl.BlockSpec((1, n_tiles, HEAD_DIM, t), lambda h, i: (h, 0, 0, 0)),
            pl.BlockSpec((1, N_BIAS_TILES, t, t), lambda h, i: (h, 0, 0, 0)),
            pl.BlockSpec((tq, HEAD_DIM), lambda h, i: (i, gblk + h)),
            pl.BlockSpec((1, HEAD_DIM), lambda h, i: (0, h)),
        ],
        out_specs=pl.BlockSpec((tq, HEAD_DIM), lambda h, i: (i, h)),
        out_shape=jax.ShapeDtypeStruct((s, D_DIFF), BF16),
        scratch_shapes=[
            pltpu.VMEM((HEAD_DIM, 2 * tq), BF16),
            pltpu.VMEM((1, 2 * tq), F32),
            pltpu.VMEM((1, 2 * tq), F32),
            pltpu.VMEM((HEAD_DIM, 2 * tq), F32),
            pltpu.VMEM((HEAD_DIM, 8), F32),
            pltpu.VMEM((3, 1, 2 * tq), F32),
            pltpu.VMEM((t, 2 * tq), BF16),
            pltpu.VMEM((t, 2 * tq), BF16),
        ],
        compiler_params=pltpu.CompilerParams(
            dimension_semantics=("arbitrary", "arbitrary"), vmem_limit_bytes=VMEM_LIMIT),
        name="diff_attention",
    )(bnd, lq, qt4, proj, vt4, bias, proj, gain)


def _outproj_kernel(ya_ref, ag_ref, yb_ref, u_ref, vsg_ref, cg_ref, x_ref, wout_ref, ws_ref,
                    bs_ref, vgain_ref, pgain_ref, o_ref, y_scr):
    tm = x_ref.shape[0]
    y_scr[:, 0:D_FOURIER] = (ya_ref[...] * _silu(ag_ref[...].astype(F32))).astype(BF16)
    y_scr[:, D_FOURIER:D_FOURIER + D_DIFF] = yb_ref[...]
    for g in range(D_GMLP // GROUP):
        cols = slice(g * GROUP, (g + 1) * GROUP)
        v = vsg_ref[:, cols].astype(F32)
        ms = jnp.mean(v * v, axis=-1, keepdims=True)
        vn = ((v * lax.rsqrt(ms + EPS)) * vgain_ref[:, cols]).astype(BF16)
        w = ws_ref[g]
        b = bs_ref[g]
        for c in range(tm // CHUNK):
            rows = slice(c * CHUNK, (c + 1) * CHUNK)
            mixed = jnp.dot(w, vn[rows], preferred_element_type=F32) + b
            gate = cg_ref[rows, cols].astype(F32)
            yc = (u_ref[rows, cols].astype(F32) * mixed) * _silu(gate)
            y_scr[rows, D_FOURIER + D_DIFF + g * GROUP:D_FOURIER + D_DIFF + (g + 1) * GROUP] = (
                yc.astype(BF16))
    y = jnp.dot(y_scr[...], wout_ref[...], preferred_element_type=F32)
    ms = jnp.mean(y * y, axis=-1, keepdims=True)
    o_ref[...] = x_ref[...] + (y * lax.rsqrt(ms + EPS)) * pgain_ref[...]


def _outproj(ya, yb, proj, x2, wout_bf, ws_bf, layer, bs3, vgain, pgain):
    s, d = x2.shape
    tm = TM_OUT
    ablk = OFF_A_GATE // D_FOURIER
    ublk, vblk, cblk = OFF_U // D_GMLP, OFF_VSG // D_GMLP, OFF_C_GATE // D_GMLP
    const = lambda shape: pl.BlockSpec(shape, lambda i: tuple(0 for _ in shape))
    return pl.pallas_call(
        _outproj_kernel,
        grid=(s // tm,),
        in_specs=[
            pl.BlockSpec((tm, D_FOURIER), lambda i: (i, 0)),
            pl.BlockSpec((tm, D_FOURIER), lambda i: (i, ablk)),
            pl.BlockSpec((tm, D_DIFF), lambda i: (i, 0)),
            pl.BlockSpec((tm, D_GMLP), lambda i: (i, ublk)),
            pl.BlockSpec((tm, D_GMLP), lambda i: (i, vblk)),
            pl.BlockSpec((tm, D_GMLP), lambda i: (i, cblk)),
            pl.BlockSpec((tm, d), lambda i: (i, 0)),
            pl.BlockSpec((None, d, d), lambda i: (layer, 0, 0)),
            pl.BlockSpec((None,) + ws_bf.shape[1:], lambda i: (layer, 0, 0, 0)),
            const(bs3.shape),
            const((1, D_GMLP)),
            const((1, d)),
        ],
        out_specs=pl.BlockSpec((tm, d), lambda i: (i, 0)),
        out_shape=jax.ShapeDtypeStruct((s, d), F32),
        scratch_shapes=[pltpu.VMEM((tm, d), BF16)],
        compiler_params=pltpu.CompilerParams(
            dimension_semantics=("arbitrary",), vmem_limit_bytes=VMEM_LIMIT),
        name="outproj",
    )(ya, proj, yb, proj, proj, proj, x2, wout_bf, ws_bf, bs3, vgain, pgain)


def kernel(x, w_in, pre_gain, post_gain, w_fourier, lambda_qk, diff_out_gain, sg_v_gain,
           w_spatial, b_spatial, w_out, rel_bias):
    b, s, d = x.shape
    depth = w_in.shape[0]
    assert b == 1 and w_in.shape[2] == D_IN
    x2 = x.reshape(s, d)

    q_scale = (HALF_DIM ** -0.5) * LOG2E
    colscale = jnp.ones((1, D_IN), F32).at[:, OFF_Q:OFF_Q + D_DIFF].set(q_scale)
    bias = _bias_tiles(rel_bias, T_ATT)
    bias_bounds = jnp.stack([jnp.max(rel_bias, axis=0), jnp.min(rel_bias, axis=0)], axis=1) * LOG2E
    consts = _dft_consts(s // FFT_N2, FFT_N2, GROUP)

    w_in_bf, w_out_bf = w_in.astype(BF16), w_out.astype(BF16)
    w_fourier_bf, w_spatial_bf = w_fourier.astype(BF16), w_spatial.astype(BF16)

    for l in range(depth):
        lambda_init = 0.8 - 0.6 * math.exp(-0.3 * l)
        proj, a_in, qt4, vt4 = _inproj(x2, pre_gain[l].reshape(1, d), w_in_bf, l, colscale)
        ya = _fourier_branch(a_in, w_fourier_bf, l, consts)
        yb = _attention(bias_bounds, lambda_qk[l], qt4, proj, vt4, bias,
                        diff_out_gain[l].reshape(1, D_DIFF), lambda_init)
        x2 = _outproj(ya, yb, proj, x2, w_out_bf, w_spatial_bf, l,
                      b_spatial[l].reshape(D_GMLP // GROUP, CHUNK, 1),
                      sg_v_gain[l].reshape(1, D_GMLP), post_gain[l].reshape(1, d))
    return x2.reshape(b, s, d)
```

```python
import functools
import math

import numpy as np
import jax
import jax.numpy as jnp
from jax import lax
from jax.experimental import pallas as pl
from jax.experimental.pallas import tpu as pltpu

F32 = jnp.float32
BF16 = jnp.bfloat16

EPS = 1e-6
LOG2E = math.log2(math.e)
NEG_INIT = -1e30
EXP2_SPAN = 120.0

D_FOURIER = 512
D_DIFF = 1024
D_GMLP = 512
GROUP = 128
HEAD_DIM = 128
HALF_DIM = 64
N_HEADS = 8
CHUNK = 128
N_BUCKETS = 32
MAX_DISTANCE = 128
OFF_A_IN, OFF_A_GATE, OFF_Q, OFF_K, OFF_V, OFF_B_GATE, OFF_U, OFF_VSG, OFF_C_GATE = (
    0, 512, 1024, 2048, 3072, 4096, 5120, 5632, 6144)
D_IN = 6656

MXU_COLS = 256
TM_IN = 512
TN_IN = 1664
TM_OUT = 512
T_ATT = 512
TQ_ATT = 2048
N_BIAS_TILES = 4
ZERO_BIAS_TILE = 3
BIAS_COL_SHIFT = 256
FFT_N2 = 128
FFT_ROWS = 8
VMEM_LIMIT = 56 * 1024 * 1024
assert TM_IN % T_ATT == 0


def _silu(x):
    return x * (1.0 / (1.0 + jnp.exp(-x)))


def _inproj_kernel(x_ref, g_ref, w_ref, cs_ref, proj_ref, ain_ref, qt_ref, vt_ref, a_scr):
    j = pl.program_id(1)

    @pl.when(j == 0)
    def _():
        x = x_ref[...]
        ms = jnp.mean(x * x, axis=-1, keepdims=True)
        a_scr[...] = ((x * lax.rsqrt(ms + EPS)) * g_ref[...]).astype(BF16)

    acc = jnp.dot(a_scr[...], w_ref[...], preferred_element_type=F32)
    scaled = acc * cs_ref[...]
    proj_ref[...] = scaled.astype(BF16)

    @pl.when(j == 0)
    def _():
        ain_ref[...] = acc[:, :D_FOURIER]

    tn = proj_ref.shape[1]
    for jj in range(D_IN // tn):
        heads = [(dst, (c - off) // HEAD_DIM, c - jj * tn)
                 for dst, off in ((qt_ref, OFF_Q), (vt_ref, OFF_V))
                 for c in range(off, off + D_DIFF, HEAD_DIM)
                 if jj * tn <= c < (jj + 1) * tn]
        if heads:
            @pl.when(j == jj)
            def _(heads=heads):
                for dst, h, c0 in heads:
                    for r in range(dst.shape[1]):
                        rows = slice(r * T_ATT, (r + 1) * T_ATT)
                        dst[h, r] = scaled[rows, c0:c0 + HEAD_DIM].T.astype(BF16)


def _inproj(x2, gain, w_bf, layer, colscale):
    s, d = x2.shape
    n = w_bf.shape[2]
    grid = (s // TM_IN, n // TN_IN)
    head_tiles = pl.BlockSpec((N_HEADS, TM_IN // T_ATT, HEAD_DIM, T_ATT),
                              lambda i, j: (0, i, 0, 0))
    return pl.pallas_call(
        _inproj_kernel,
        grid=grid,
        in_specs=[
            pl.BlockSpec((TM_IN, d), lambda i, j: (i, 0)),
            pl.BlockSpec((1, d), lambda i, j: (0, 0)),
            pl.BlockSpec((None, d, TN_IN), lambda i, j: (layer, 0, j)),
            pl.BlockSpec((1, TN_IN), lambda i, j: (0, j)),
        ],
        out_specs=[
            pl.BlockSpec((TM_IN, TN_IN), lambda i, j: (i, j)),
            pl.BlockSpec((TM_IN, D_FOURIER), lambda i, j: (i, 0)),
            head_tiles,
            head_tiles,
        ],
        out_shape=[
            jax.ShapeDtypeStruct((s, n), BF16),
            jax.ShapeDtypeStruct((s, D_FOURIER), F32),
            jax.ShapeDtypeStruct((N_HEADS, s // T_ATT, HEAD_DIM, T_ATT), BF16),
            jax.ShapeDtypeStruct((N_HEADS, s // T_ATT, HEAD_DIM, T_ATT), BF16),
        ],
        scratch_shapes=[pltpu.VMEM((TM_IN, d), BF16)],
        compiler_params=pltpu.CompilerParams(
            dimension_semantics=("arbitrary", "arbitrary"),
            vmem_limit_bytes=VMEM_LIMIT),
        name="inproj",
    )(x2, gain, w_bf, colscale)


def _dft_consts(n1, n2, c):
    def cs(n):
        idx = np.arange(n)
        ang = 2.0 * np.pi * ((idx[:, None] * idx[None, :]) % n) / n
        return np.cos(ang), np.sin(ang)

    c1, s1 = cs(n1)
    c2, s2 = cs(n2)
    cc, sc = cs(c)
    f1 = np.kron(np.concatenate([c1, -s1], axis=0), np.eye(FFT_ROWS))
    f2 = np.block([[c2, s2], [s2, -c2]])
    n = n1 * n2
    k1 = np.arange(n1)[:, None]
    m2 = np.arange(n2)[None, :]
    ang = 2.0 * np.pi * ((k1 * m2) % n) / n
    tw_c = np.cos(ang)[:, :, None]
    tw_s = np.sin(ang)[:, :, None]
    ch = np.concatenate([cc, -sc], axis=0)
    f32 = lambda a: jnp.asarray(a.astype(np.float32))
    return f32(f1), f32(f2), f32(tw_c), f32(tw_s), f32(ch)


def _fft1_kernel(f1_ref, x_ref, y_ref):
    n1, rows, c = x_ref.shape
    x = x_ref[...].reshape(n1 * rows, c).astype(BF16)
    y = jnp.dot(f1_ref[...].astype(BF16), x, preferred_element_type=F32)
    y_ref[...] = y.reshape(y_ref.shape)


def _fft1(f1, x3):
    n1, n2, c = x3.shape
    return pl.pallas_call(
        _fft1_kernel,
        grid=(n2 // FFT_ROWS,),
        in_specs=[pl.BlockSpec(f1.shape, lambda j: (0, 0)),
                  pl.BlockSpec((n1, FFT_ROWS, c), lambda j: (0, j, 0))],
        out_specs=pl.BlockSpec((2 * n1, FFT_ROWS, c), lambda j: (0, j, 0)),
        out_shape=jax.ShapeDtypeStruct((2 * n1, n2, c), F32),
        compiler_params=pltpu.CompilerParams(
            dimension_semantics=("arbitrary",), vmem_limit_bytes=VMEM_LIMIT),
        name="fft_stage1",
    )(f1, x3)


def _fft2_kernel(y_ref, twc_ref, tws_ref, f2_ref, ch_ref, wf_ref, o_ref, *, scale):
    f2 = f2_ref[...].astype(BF16)
    ch = ch_ref[...].astype(BF16)
    _, rows, n2, c = y_ref.shape
    ps, qs = [], []
    for r in range(rows):
        yr = y_ref[0, r]
        yi = y_ref[1, r]
        tc = twc_ref[r]
        ts = tws_ref[r]
        zr = yr * tc + yi * ts
        zi = yi * tc - yr * ts
        z = jnp.concatenate([zr, zi], axis=0).astype(BF16)
        pq = jnp.dot(f2, z, preferred_element_type=F32)
        ps.append(pq[:n2].astype(BF16))
        qs.append(pq[n2:].astype(BF16))
    p = jnp.concatenate(ps, axis=0)
    q = jnp.concatenate(qs, axis=0)
    fs = []
    for g in range(c // GROUP):
        cols = slice(g * GROUP, (g + 1) * GROUP)
        pq_g = jnp.concatenate([p[:, cols], q[:, cols]], axis=1)
        fs.append(jnp.dot(pq_g, ch, preferred_element_type=F32))
    f = (jnp.concatenate(fs, axis=1) * scale).astype(BF16)
    ya = jnp.dot(f, wf_ref[...], preferred_element_type=F32)
    for r in range(rows):
        o_ref[:, r, :] = ya[r * n2:(r + 1) * n2]


def _fft2(y4, twc, tws, f2, ch, wf_bf, layer, scale):
    _, n1, n2, c = y4.shape
    const = lambda shape: pl.BlockSpec(shape, lambda k: tuple(0 for _ in shape))
    return pl.pallas_call(
        functools.partial(_fft2_kernel, scale=scale),
        grid=(n1 // FFT_ROWS,),
        in_specs=[
            pl.BlockSpec((2, FFT_ROWS, n2, c), lambda k: (0, k, 0, 0)),
            pl.BlockSpec((FFT_ROWS, n2, 1), lambda k: (k, 0, 0)),
            pl.BlockSpec((FFT_ROWS, n2, 1), lambda k: (k, 0, 0)),
            const((2 * n2, 2 * n2)),
            const(ch.shape),
            pl.BlockSpec((None, c, c), lambda k: (layer, 0, 0)),
        ],
        out_specs=pl.BlockSpec((n2, FFT_ROWS, c), lambda k: (0, k, 0)),
        out_shape=jax.ShapeDtypeStruct((n2, n1, c), F32),
        compiler_params=pltpu.CompilerParams(
            dimension_semantics=("arbitrary",), vmem_limit_bytes=VMEM_LIMIT),
        name="fft_stage2",
    )(y4, twc, tws, f2, ch, wf_bf)


def _fourier_branch(a_in, wf_bf, layer, consts):
    s, c = a_in.shape
    n2 = FFT_N2
    n1 = s // n2
    f1, f2, twc, tws, ch = consts
    y = _fft1(f1, a_in.reshape(n1, n2, c))
    scale = 1.0 / math.sqrt(s * GROUP)
    out = _fft2(y.reshape(2, n1, n2, c), twc, tws, f2, ch, wf_bf, layer, scale)
    return out.reshape(s, c)


def _t5_bucket(rel):
    nb = N_BUCKETS // 2
    ret = (rel > 0).astype(jnp.int32) * nb
    n = jnp.abs(rel)
    max_exact = nb // 2
    nf = jnp.maximum(n, 1).astype(jnp.float32)
    large = max_exact + (jnp.log(nf / max_exact) / math.log(MAX_DISTANCE / max_exact)
                         * (nb - max_exact)).astype(jnp.int32)
    large = jnp.minimum(large, nb - 1)
    return ret + jnp.where(n < max_exact, n, large)


def _bias_kernel(rb_ref, bk_ref, o_ref):
    h = pl.program_id(0)
    half = N_BUCKETS // 2
    bucket_ranges = (range(0, half), range(0, N_BUCKETS), range(half, N_BUCKETS), range(0))
    for d, buckets in enumerate(bucket_ranges):
        bk = bk_ref[d]
        acc = jnp.zeros(bk.shape, F32)
        for b in buckets:
            acc = jnp.where(bk == b, rb_ref[b, h], acc)
        o_ref[0, d] = acc * LOG2E


def _bias_tiles(rel_bias, t):
    kk = jnp.arange(t, dtype=jnp.int32)[:, None]
    rr = ((jnp.arange(t, dtype=jnp.int32) - BIAS_COL_SHIFT) % t)[None, :]
    rel = jnp.stack([d * t + kk - rr for d in (-1, 0, 1)])
    bucket = jnp.concatenate([_t5_bucket(rel), jnp.full((1, t, t), -1, jnp.int32)])
    return pl.pallas_call(
        _bias_kernel,
        grid=(N_HEADS,),
        in_specs=[pl.BlockSpec(memory_space=pltpu.SMEM),
                  pl.BlockSpec((N_BIAS_TILES, t, t), lambda h: (0, 0, 0))],
        out_specs=pl.BlockSpec((1, N_BIAS_TILES, t, t), lambda h: (h, 0, 0, 0)),
        out_shape=jax.ShapeDtypeStruct((N_HEADS, N_BIAS_TILES, t, t), F32),
        compiler_params=pltpu.CompilerParams(
            dimension_semantics=("arbitrary",), vmem_limit_bytes=VMEM_LIMIT),
        name="bias_tiles",
    )(rel_bias, bucket)


def _attn_kernel(bnd_ref, lq_ref, qt_ref, k_ref, vt_ref, bias_ref, gate_ref, gain_ref, o_ref,
                 q2_scr, m_scr, l_scr, acc_scr, kabs_scr, mreg_scr, pa_scr, pb_scr,
                 *, lambda_init, n_tiles):
    tq = qt_ref.shape[1] * qt_ref.shape[3]
    t = vt_ref.shape[-1]
    nqb = tq // t
    h = pl.program_id(0)
    i = pl.program_id(1)

    @pl.when(i == 0)
    def _():
        ka = jnp.max(jnp.abs(k_ref[...].astype(F32)), axis=0, keepdims=True)
        kabs_scr[...] = jnp.broadcast_to(ka, (8, HEAD_DIM)).T

    qt = jnp.concatenate([qt_ref[0, r] for r in range(qt_ref.shape[1])], axis=1)
    row = lax.broadcasted_iota(jnp.int32, qt.shape, 0)
    zero = jnp.zeros_like(qt)
    q2_scr[:, :tq] = jnp.where(row < HALF_DIM, qt, zero)
    q2_scr[:, tq:] = jnp.where(row >= HALF_DIM, qt, zero)
    l_scr[...] = jnp.zeros(l_scr.shape, F32)
    acc_scr[...] = jnp.zeros(acc_scr.shape, F32)

    cl_col = (t - 1 + BIAS_COL_SHIFT) % t
    cr_col = BIAS_COL_SHIFT % t
    c_left = bias_ref[0, 0, 0:1, cl_col:cl_col + 1]
    c_right = bias_ref[0, 2, t - 1:t, cr_col:cr_col + 1]

    wq = jnp.abs(qt.astype(F32)) * kabs_scr[:, 0:1]
    hb = jnp.concatenate([jnp.sum(wq[:HALF_DIM], axis=0, keepdims=True),
                          jnp.sum(wq[HALF_DIM:], axis=0, keepdims=True)], axis=1)
    b_max = bnd_ref[h, 0]
    b_min = bnd_ref[h, 1]
    fixed_ok = 2.0 * jnp.max(hb) + (b_max - b_min) <= EXP2_SPAN

    def block_relation(j, col0):
        e = j - (nqb * i + (col0 % tq) // t)
        tile_idx = jnp.where(jnp.abs(e) <= 1, e + 1, ZERO_BIAS_TILE)
        region = jnp.where(e < -1, 0, jnp.where(e > 1, 2, 1))
        return tile_idx, region

    def step(j_p, p_out, j_v, p_in, with_bias):
        if j_p is not None:
            kt = k_ref[pl.ds(pl.multiple_of(j_p * t, t), t), :]
        if j_v is not None:
            vt = vt_ref[0, j_v]
        for c0 in range(0, 2 * tq, MXU_COLS):
            cs = slice(c0, c0 + MXU_COLS)
            if j_p is not None:
                tile_idx, region = block_relation(j_p, c0)
                s = jnp.dot(kt, q2_scr[:, cs], preferred_element_type=F32)
                if with_bias:
                    b0 = (c0 + BIAS_COL_SHIFT) % t
                    s = s + bias_ref[0, tile_idx, :, b0:b0 + MXU_COLS]
                p = jnp.exp2(s - mreg_scr[region, :, cs])
                l_scr[:, cs] += jnp.sum(p, axis=0, keepdims=True)
                p_out[:, cs] = p.astype(BF16)
            if j_v is not None:
                acc_scr[:, cs] += jnp.dot(vt, p_in[:, cs], preferred_element_type=F32)

    def near_band(j):
        return jnp.logical_and(j >= nqb * i - 1, j <= nqb * i + nqb)

    def step_pair(j, last):
        def run(with_bias):
            step(j + 1, pb_scr, j, pa_scr, with_bias)
            step(None if last else j + 2, pa_scr, j + 1, pb_scr, with_bias)

        near = near_band(j + 1) if last else jnp.logical_or(near_band(j + 1), near_band(j + 2))
        lax.cond(near, lambda: run(True), lambda: run(False))

    @pl.when(fixed_ok)
    def _():
        m0 = hb + b_max
        mreg_scr[0] = m0 - c_left
        mreg_scr[1] = m0
        mreg_scr[2] = m0 - c_right

        def pair_body(jj, carry):
            step_pair(2 * jj, last=False)
            return carry

        step(0, pa_scr, None, None, True)
        lax.fori_loop(0, n_tiles // 2 - 1, pair_body, 0)
        step_pair(n_tiles - 2, last=True)

    def online_step(j):
        kt = k_ref[pl.ds(pl.multiple_of(j * t, t), t), :]
        vt = vt_ref[0, j]
        for c0 in range(0, 2 * tq, t):
            cs = slice(c0, c0 + t)
            tile_idx, region = block_relation(j, c0)
            c_far = jnp.where(region == 0, c_left, jnp.where(region == 2, c_right, 0.0))
            b = bias_ref[0, tile_idx]
            b = jnp.concatenate([b[:, BIAS_COL_SHIFT:], b[:, :BIAS_COL_SHIFT]], axis=1)
            s = jnp.dot(kt, q2_scr[:, cs], preferred_element_type=F32) + (b + c_far)
            m_old = m_scr[:, cs]
            m_new = jnp.maximum(m_old, jnp.max(s, axis=0, keepdims=True))
            p = jnp.exp2(s - m_new)
            alpha = jnp.exp2(m_old - m_new)
            l_scr[:, cs] = alpha * l_scr[:, cs] + jnp.sum(p, axis=0, keepdims=True)
            pv = jnp.dot(vt, p.astype(BF16), preferred_element_type=F32)
            acc_scr[:, cs] = alpha * acc_scr[:, cs] + pv
            m_scr[:, cs] = m_new

    @pl.when(jnp.logical_not(fixed_ok))
    def _():
        def body(j, carry):
            online_step(j)
            return carry

        m_scr[...] = jnp.full(m_scr.shape, NEG_INIT, F32)
        lax.fori_loop(0, n_tiles, body, 0)

    lq = lq_ref[...]
    lam = (jnp.exp(jnp.sum(lq[0:1] * lq[1:2], axis=-1, keepdims=True))
           - jnp.exp(jnp.sum(lq[2:3] * lq[3:4], axis=-1, keepdims=True)) + lambda_init)
    accn = acc_scr[...] * (1.0 / l_scr[...])
    o_t = accn[:, :tq] - lam * accn[:, tq:]
    o = o_t.T
    ms = jnp.mean(o * o, axis=-1, keepdims=True)
    y = ((o * lax.rsqrt(ms + EPS)) * gain_ref[...]) * (1.0 - lambda_init)
    o_ref[...] = (y * _silu(gate_ref[...].astype(F32))).astype(BF16)


def _attention(bnd, lq, qt4, proj, vt4, bias, gain, lambda_init):
    s = proj.shape[0]
    tq = TQ_ATT
    t = vt4.shape[-1]
    assert t == bias.shape[-1] and tq % t == 0
    n_tiles = s // t
    kblk = OFF_K // HEAD_DIM
    gblk = OFF_B_GATE // HEAD_DIM
    kern = functools.partial(_attn_kernel, lambda_init=lambda_init, n_tiles=n_tiles)
    return pl.pallas_call(
        kern,
        grid=(N_HEADS, s // tq),
        in_specs=[
            pl.BlockSpec(memory_space=pltpu.SMEM),
            pl.BlockSpec((4, HALF_DIM), lambda h, i: (0, 0)),
            pl.BlockSpec((1, tq // t, HEAD_DIM, t), lambda h, i: (h, i, 0, 0)),
            pl.BlockSpec((s, HEAD_DIM), lambda h, i: (0, kblk + h)),
            pl.BlockSpec((1, n_tiles, HEAD_DIM, t), lambda h, i: (h, 0, 0, 0)),
            pl.BlockSpec((1, N_BIAS_TILES, t, t), lambda h, i: (h, 0, 0, 0)),
            pl.BlockSpec((tq, HEAD_DIM), lambda h, i: (i, gblk + h)),
            pl.BlockSpec((1, HEAD_DIM), lambda h, i: (0, h)),
        ],
        out_specs=pl.BlockSpec((tq, HEAD_DIM), lambda h, i: (i, h)),
        out_shape=jax.ShapeDtypeStruct((s, D_DIFF), BF16),
        scratch_shapes=[
            pltpu.VMEM((HEAD_DIM, 2 * tq), BF16),
            pltpu.VMEM((1, 2 * tq), F32),
            pltpu.VMEM((1, 2 * tq), F32),
            pltpu.VMEM((HEAD_DIM, 2 * tq), F32),
            pltpu.VMEM((HEAD_DIM, 8), F32),
            pltpu.VMEM((3, 1, 2 * tq), F32),
            pltpu.VMEM((t, 2 * tq), BF16),
            pltpu.VMEM((t, 2 * tq), BF16),
        ],
        compiler_params=pltpu.CompilerParams(
            dimension_semantics=("arbitrary", "arbitrary"), vmem_limit_bytes=VMEM_LIMIT),
        name="diff_attention",
    )(bnd, lq, qt4, proj, vt4, bias, proj, gain)


def _outproj_kernel(ya_ref, ag_ref, yb_ref, u_ref, vsg_ref, cg_ref, x_ref, wout_ref, ws_ref,
                    bs_ref, vgain_ref, pgain_ref, o_ref, y_scr):
    tm = x_ref.shape[0]
    y_scr[:, 0:D_FOURIER] = (ya_ref[...] * _silu(ag_ref[...].astype(F32))).astype(BF16)
    y_scr[:, D_FOURIER:D_FOURIER + D_DIFF] = yb_ref[...]
    for g in range(D_GMLP // GROUP):
        cols = slice(g * GROUP, (g + 1) * GROUP)
        v = vsg_ref[:, cols].astype(F32)
        ms = jnp.mean(v * v, axis=-1, keepdims=True)
        vn = ((v * lax.rsqrt(ms + EPS)) * vgain_ref[:, cols]).astype(BF16)
        w = ws_ref[g]
        b = bs_ref[g]
        for c in range(tm // CHUNK):
            rows = slice(c * CHUNK, (c + 1) * CHUNK)
            mixed = jnp.dot(w, vn[rows], preferred_element_type=F32) + b
            gate = cg_ref[rows, cols].astype(F32)
            yc = (u_ref[rows, cols].astype(F32) * mixed) * _silu(gate)
            y_scr[rows, D_FOURIER + D_DIFF + g * GROUP:D_FOURIER + D_DIFF + (g + 1) * GROUP] = (
                yc.astype(BF16))
    y = jnp.dot(y_scr[...], wout_ref[...], preferred_element_type=F32)
    ms = jnp.mean(y * y, axis=-1, keepdims=True)
    o_ref[...] = x_ref[...] + (y * lax.rsqrt(ms + EPS)) * pgain_ref[...]


def _outproj(ya, yb, proj, x2, wout_bf, ws_bf, layer, bs3, vgain, pgain):
    s, d = x2.shape
    tm = TM_OUT
    ablk = OFF_A_GATE // D_FOURIER
    ublk, vblk, cblk = OFF_U // D_GMLP, OFF_VSG // D_GMLP, OFF_C_GATE // D_GMLP
    const = lambda shape: pl.BlockSpec(shape, lambda i: tuple(0 for _ in shape))
    return pl.pallas_call(
        _outproj_kernel,
        grid=(s // tm,),
        in_specs=[
            pl.BlockSpec((tm, D_FOURIER), lambda i: (i, 0)),
            pl.BlockSpec((tm, D_FOURIER), lambda i: (i, ablk)),
            pl.BlockSpec((tm, D_DIFF), lambda i: (i, 0)),
            pl.BlockSpec((tm, D_GMLP), lambda i: (i, ublk)),
            pl.BlockSpec((tm, D_GMLP), lambda i: (i, vblk)),
            pl.BlockSpec((tm, D_GMLP), lambda i: (i, cblk)),
            pl.BlockSpec((tm, d), lambda i: (i, 0)),
            pl.BlockSpec((None, d, d), lambda i: (layer, 0, 0)),
            pl.BlockSpec((None,) + ws_bf.shape[1:], lambda i: (layer, 0, 0, 0)),
            const(bs3.shape),
            const((1, D_GMLP)),
            const((1, d)),
        ],
        out_specs=pl.BlockSpec((tm, d), lambda i: (i, 0)),
        out_shape=jax.ShapeDtypeStruct((s, d), F32),
        scratch_shapes=[pltpu.VMEM((tm, d), BF16)],
        compiler_params=pltpu.CompilerParams(
            dimension_semantics=("arbitrary",), vmem_limit_bytes=VMEM_LIMIT),
        name="outproj",
    )(ya, proj, yb, proj, proj, proj, x2, wout_bf, ws_bf, bs3, vgain, pgain)


def kernel(x, w_in, pre_gain, post_gain, w_fourier, lambda_qk, diff_out_gain, sg_v_gain,
           w_spatial, b_spatial, w_out, rel_bias):
    b, s, d = x.shape
    depth = w_in.shape[0]
    assert b == 1 and w_in.shape[2] == D_IN
    x2 = x.reshape(s, d)

    q_scale = (HALF_DIM ** -0.5) * LOG2E
    colscale = jnp.ones((1, D_IN), F32).at[:, OFF_Q:OFF_Q + D_DIFF].set(q_scale)
    bias = _bias_tiles(rel_bias, T_ATT)
    bias_bounds = jnp.stack([jnp.max(rel_bias, axis=0), jnp.min(rel_bias, axis=0)], axis=1) * LOG2E
    consts = _dft_consts(s // FFT_N2, FFT_N2, GROUP)

    w_in_bf, w_out_bf = w_in.astype(BF16), w_out.astype(BF16)
    w_fourier_bf, w_spatial_bf = w_fourier.astype(BF16), w_spatial.astype(BF16)

    for l in range(depth):
        lambda_init = 0.8 - 0.6 * math.exp(-0.3 * l)
        proj, a_in, qt4, vt4 = _inproj(x2, pre_gain[l].reshape(1, d), w_in_bf, l, colscale)
        ya = _fourier_branch(a_in, w_fourier_bf, l, consts)
        yb = _attention(bias_bounds, lambda_qk[l], qt4, proj, vt4, bias,
                        diff_out_gain[l].reshape(1, D_DIFF), lambda_init)
        x2 = _outproj(ya, yb, proj, x2, w_out_bf, w_spatial_bf, l,
                      b_spatial[l].reshape(D_GMLP // GROUP, CHUNK, 1),
                      sg_v_gain[l].reshape(1, D_GMLP), post_gain[l].reshape(1, d))
    return x2.reshape(b, s, d)
```

```python
import functools
import math

import numpy as np
import jax
import jax.numpy as jnp
from jax import lax
from jax.experimental import pallas as pl
from jax.experimental.pallas import tpu as pltpu

F32 = jnp.float32
BF16 = jnp.bfloat16

EPS = 1e-6
LOG2E = math.log2(math.e)
NEG_INIT = -1e30
EXP2_SPAN = 120.0

D_FOURIER = 512
D_DIFF = 1024
D_GMLP = 512
GROUP = 128
HEAD_DIM = 128
HALF_DIM = 64
N_HEADS = 8
CHUNK = 128
N_BUCKETS = 32
MAX_DISTANCE = 128
OFF_A_IN, OFF_A_GATE, OFF_Q, OFF_K, OFF_V, OFF_B_GATE, OFF_U, OFF_VSG, OFF_C_GATE = (
    0, 512, 1024, 2048, 3072, 4096, 5120, 5632, 6144)
D_IN = 6656

MXU_COLS = 256
TM_IN = 512
TN_IN = 1664
TM_OUT = 512
T_ATT = 512
TQ_ATT = 2048
N_BIAS_TILES = 4
ZERO_BIAS_TILE = 3
FFT_N2 = 128
FFT_ROWS = 8
VMEM_LIMIT = 56 * 1024 * 1024
assert TM_IN % T_ATT == 0


def _silu(x):
    return x * (1.0 / (1.0 + jnp.exp(-x)))


def _inproj_kernel(x_ref, g_ref, w_ref, cs_ref, proj_ref, ain_ref, qt_ref, vt_ref, a_scr):
    j = pl.program_id(1)

    @pl.when(j == 0)
    def _():
        x = x_ref[...]
        ms = jnp.mean(x * x, axis=-1, keepdims=True)
        a_scr[...] = ((x * lax.rsqrt(ms + EPS)) * g_ref[...]).astype(BF16)

    acc = jnp.dot(a_scr[...], w_ref[...], preferred_element_type=F32)
    scaled = acc * cs_ref[...]
    proj_ref[...] = scaled.astype(BF16)

    @pl.when(j == 0)
    def _():
        ain_ref[...] = acc[:, :D_FOURIER]

    tn = proj_ref.shape[1]
    for jj in range(D_IN // tn):
        heads = [(dst, (c - off) // HEAD_DIM, c - jj * tn)
                 for dst, off in ((qt_ref, OFF_Q), (vt_ref, OFF_V))
                 for c in range(off, off + D_DIFF, HEAD_DIM)
                 if jj * tn <= c < (jj + 1) * tn]
        if heads:
            @pl.when(j == jj)
            def _(heads=heads):
                for dst, h, c0 in heads:
                    for r in range(dst.shape[1]):
                        rows = slice(r * T_ATT, (r + 1) * T_ATT)
                        dst[h, r] = scaled[rows, c0:c0 + HEAD_DIM].T.astype(BF16)


def _inproj(x2, gain, w_bf, layer, colscale):
    s, d = x2.shape
    n = w_bf.shape[2]
    grid = (s // TM_IN, n // TN_IN)
    head_tiles = pl.BlockSpec((N_HEADS, TM_IN // T_ATT, HEAD_DIM, T_ATT),
                              lambda i, j: (0, i, 0, 0))
    return pl.pallas_call(
        _inproj_kernel,
        grid=grid,
        in_specs=[
            pl.BlockSpec((TM_IN, d), lambda i, j: (i, 0)),
            pl.BlockSpec((1, d), lambda i, j: (0, 0)),
            pl.BlockSpec((None, d, TN_IN), lambda i, j: (layer, 0, j)),
            pl.BlockSpec((1, TN_IN), lambda i, j: (0, j)),
        ],
        out_specs=[
            pl.BlockSpec((TM_IN, TN_IN), lambda i, j: (i, j)),
            pl.BlockSpec((TM_IN, D_FOURIER), lambda i, j: (i, 0)),
            head_tiles,
            head_tiles,
        ],
        out_shape=[
            jax.ShapeDtypeStruct((s, n), BF16),
            jax.ShapeDtypeStruct((s, D_FOURIER), F32),
            jax.ShapeDtypeStruct((N_HEADS, s // T_ATT, HEAD_DIM, T_ATT), BF16),
            jax.ShapeDtypeStruct((N_HEADS, s // T_ATT, HEAD_DIM, T_ATT), BF16),
        ],
        scratch_shapes=[pltpu.VMEM((TM_IN, d), BF16)],
        compiler_params=pltpu.CompilerParams(
            dimension_semantics=("arbitrary", "arbitrary"),
            vmem_limit_bytes=VMEM_LIMIT),
        name="inproj",
    )(x2, gain, w_bf, colscale)


def _dft_consts(n1, n2, c):
    def cs(n):
        idx = np.arange(n)
        ang = 2.0 * np.pi * ((idx[:, None] * idx[None, :]) % n) / n
        return np.cos(ang), np.sin(ang)

    c1, s1 = cs(n1)
    c2, s2 = cs(n2)
    cc, sc = cs(c)
    f1 = np.kron(np.concatenate([c1, -s1], axis=0), np.eye(FFT_ROWS))
    f2 = np.block([[c2, s2], [s2, -c2]])
    n = n1 * n2
    k1 = np.arange(n1)[:, None]
    m2 = np.arange(n2)[None, :]
    ang = 2.0 * np.pi * ((k1 * m2) % n) / n
    tw_c = np.cos(ang)[:, :, None]
    tw_s = np.sin(ang)[:, :, None]
    ch = np.concatenate([cc, -sc], axis=0)
    f32 = lambda a: jnp.asarray(a.astype(np.float32))
    return f32(f1), f32(f2), f32(tw_c), f32(tw_s), f32(ch)


def _fft1_kernel(f1_ref, x_ref, y_ref):
    n1, rows, c = x_ref.shape
    x = x_ref[...].reshape(n1 * rows, c).astype(BF16)
    y = jnp.dot(f1_ref[...].astype(BF16), x, preferred_element_type=F32)
    y_ref[...] = y.reshape(y_ref.shape)


def _fft1(f1, x3):
    n1, n2, c = x3.shape
    return pl.pallas_call(
        _fft1_kernel,
        grid=(n2 // FFT_ROWS,),
        in_specs=[pl.BlockSpec(f1.shape, lambda j: (0, 0)),
                  pl.BlockSpec((n1, FFT_ROWS, c), lambda j: (0, j, 0))],
        out_specs=pl.BlockSpec((2 * n1, FFT_ROWS, c), lambda j: (0, j, 0)),
        out_shape=jax.ShapeDtypeStruct((2 * n1, n2, c), F32),
        compiler_params=pltpu.CompilerParams(
            dimension_semantics=("arbitrary",), vmem_limit_bytes=VMEM_LIMIT),
        name="fft_stage1",
    )(f1, x3)


def _fft2_kernel(y_ref, twc_ref, tws_ref, f2_ref, ch_ref, wf_ref, o_ref, *, scale):
    f2 = f2_ref[...].astype(BF16)
    ch = ch_ref[...].astype(BF16)
    _, rows, n2, c = y_ref.shape
    ps, qs = [], []
    for r in range(rows):
        yr = y_ref[0, r]
        yi = y_ref[1, r]
        tc = twc_ref[r]
        ts = tws_ref[r]
        zr = yr * tc + yi * ts
        zi = yi * tc - yr * ts
        z = jnp.concatenate([zr, zi], axis=0).astype(BF16)
        pq = jnp.dot(f2, z, preferred_element_type=F32)
        ps.append(pq[:n2].astype(BF16))
        qs.append(pq[n2:].astype(BF16))
    p = jnp.concatenate(ps, axis=0)
    q = jnp.concatenate(qs, axis=0)
    fs = []
    for g in range(c // GROUP):
        cols = slice(g * GROUP, (g + 1) * GROUP)
        pq_g = jnp.concatenate([p[:, cols], q[:, cols]], axis=1)
        fs.append(jnp.dot(pq_g, ch, preferred_element_type=F32))
    f = (jnp.concatenate(fs, axis=1) * scale).astype(BF16)
    ya = jnp.dot(f, wf_ref[...], preferred_element_type=F32)
    for r in range(rows):
        o_ref[:, r, :] = ya[r * n2:(r + 1) * n2]


def _fft2(y4, twc, tws, f2, ch, wf_bf, layer, scale):
    _, n1, n2, c = y4.shape
    const = lambda shape: pl.BlockSpec(shape, lambda k: tuple(0 for _ in shape))
    return pl.pallas_call(
        functools.partial(_fft2_kernel, scale=scale),
        grid=(n1 // FFT_ROWS,),
        in_specs=[
            pl.BlockSpec((2, FFT_ROWS, n2, c), lambda k: (0, k, 0, 0)),
            pl.BlockSpec((FFT_ROWS, n2, 1), lambda k: (k, 0, 0)),
            pl.BlockSpec((FFT_ROWS, n2, 1), lambda k: (k, 0, 0)),
            const((2 * n2, 2 * n2)),
            const(ch.shape),
            pl.BlockSpec((None, c, c), lambda k: (layer, 0, 0)),
        ],
        out_specs=pl.BlockSpec((n2, FFT_ROWS, c), lambda k: (0, k, 0)),
        out_shape=jax.ShapeDtypeStruct((n2, n1, c), F32),
        compiler_params=pltpu.CompilerParams(
            dimension_semantics=("arbitrary",), vmem_limit_bytes=VMEM_LIMIT),
        name="fft_stage2",
    )(y4, twc, tws, f2, ch, wf_bf)


def _fourier_branch(a_in, wf_bf, layer, consts):
    s, c = a_in.shape
    n2 = FFT_N2
    n1 = s // n2
    f1, f2, twc, tws, ch = consts
    y = _fft1(f1, a_in.reshape(n1, n2, c))
    scale = 1.0 / math.sqrt(s * GROUP)
    out = _fft2(y.reshape(2, n1, n2, c), twc, tws, f2, ch, wf_bf, layer, scale)
    return out.reshape(s, c)


def _t5_bucket(rel):
    nb = N_BUCKETS // 2
    ret = (rel > 0).astype(jnp.int32) * nb
    n = jnp.abs(rel)
    max_exact = nb // 2
    nf = jnp.maximum(n, 1).astype(jnp.float32)
    large = max_exact + (jnp.log(nf / max_exact) / math.log(MAX_DISTANCE / max_exact)
                         * (nb - max_exact)).astype(jnp.int32)
    large = jnp.minimum(large, nb - 1)
    return ret + jnp.where(n < max_exact, n, large)


def _bias_kernel(rb_ref, bk_ref, o_ref):
    h = pl.program_id(0)
    half = N_BUCKETS // 2
    bucket_ranges = (range(0, half), range(0, N_BUCKETS), range(half, N_BUCKETS), range(0))
    for d, buckets in enumerate(bucket_ranges):
        bk = bk_ref[d]
        acc = jnp.zeros(bk.shape, F32)
        for b in buckets:
            acc = jnp.where(bk == b, rb_ref[b, h], acc)
        o_ref[0, d] = acc * LOG2E


def _bias_tiles(rel_bias, t):
    kk = jnp.arange(t, dtype=jnp.int32)[:, None]
    rr = jnp.arange(t, dtype=jnp.int32)[None, :]
    rel = jnp.stack([d * t + kk - rr for d in (-1, 0, 1)])
    bucket = jnp.concatenate([_t5_bucket(rel), jnp.full((1, t, t), -1, jnp.int32)])
    return pl.pallas_call(
        _bias_kernel,
        grid=(N_HEADS,),
        in_specs=[pl.BlockSpec(memory_space=pltpu.SMEM),
                  pl.BlockSpec((N_BIAS_TILES, t, t), lambda h: (0, 0, 0))],
        out_specs=pl.BlockSpec((1, N_BIAS_TILES, t, t), lambda h: (h, 0, 0, 0)),
        out_shape=jax.ShapeDtypeStruct((N_HEADS, N_BIAS_TILES, t, t), F32),
        compiler_params=pltpu.CompilerParams(
            dimension_semantics=("arbitrary",), vmem_limit_bytes=VMEM_LIMIT),
        name="bias_tiles",
    )(rel_bias, bucket)


def _attn_kernel(bnd_ref, lq_ref, qt_ref, k_ref, vt_ref, bias_ref, gate_ref, gain_ref, o_ref,
                 q2_scr, m_scr, l_scr, acc_scr, kabs_scr, mreg_scr, pa_scr, pb_scr,
                 *, lambda_init, n_tiles):
    tq = qt_ref.shape[1] * qt_ref.shape[3]
    t = vt_ref.shape[-1]
    nqb = tq // t
    h = pl.program_id(0)
    i = pl.program_id(1)

    @pl.when(i == 0)
    def _():
        ka = jnp.max(jnp.abs(k_ref[...].astype(F32)), axis=0, keepdims=True)
        kabs_scr[...] = jnp.broadcast_to(ka, (8, HEAD_DIM)).T

    qt = jnp.concatenate([qt_ref[0, r] for r in range(qt_ref.shape[1])], axis=1)
    row = lax.broadcasted_iota(jnp.int32, qt.shape, 0)
    zero = jnp.zeros_like(qt)
    q2_scr[:, :tq] = jnp.where(row < HALF_DIM, qt, zero)
    q2_scr[:, tq:] = jnp.where(row >= HALF_DIM, qt, zero)
    l_scr[...] = jnp.zeros(l_scr.shape, F32)
    acc_scr[...] = jnp.zeros(acc_scr.shape, F32)

    c_left = bias_ref[0, 0, 0:1, t - 1:t]
    c_right = bias_ref[0, 2, t - 1:t, 0:1]

    wq = jnp.abs(qt.astype(F32)) * kabs_scr[:, 0:1]
    hb = jnp.concatenate([jnp.sum(wq[:HALF_DIM], axis=0, keepdims=True),
                          jnp.sum(wq[HALF_DIM:], axis=0, keepdims=True)], axis=1)
    b_max = bnd_ref[h, 0]
    b_min = bnd_ref[h, 1]
    fixed_ok = 2.0 * jnp.max(hb) + (b_max - b_min) <= EXP2_SPAN

    def block_relation(j, col0):
        e = j - (nqb * i + (col0 % tq) // t)
        tile_idx = jnp.where(jnp.abs(e) <= 1, e + 1, ZERO_BIAS_TILE)
        region = jnp.where(e < -1, 0, jnp.where(e > 1, 2, 1))
        return tile_idx, region

    def step(j_p, p_out, j_v, p_in, with_bias):
        if j_p is not None:
            kt = k_ref[pl.ds(pl.multiple_of(j_p * t, t), t), :]
        if j_v is not None:
            vt = vt_ref[0, j_v]
        for c0 in range(0, 2 * tq, MXU_COLS):
            cs = slice(c0, c0 + MXU_COLS)
            if j_p is not None:
                tile_idx, region = block_relation(j_p, c0)
                s = jnp.dot(kt, q2_scr[:, cs], preferred_element_type=F32)
                if with_bias:
                    b0 = c0 % t
                    s = s + bias_ref[0, tile_idx, :, b0:b0 + MXU_COLS]
                p = jnp.exp2(s - mreg_scr[region, :, cs])
                l_scr[:, cs] += jnp.sum(p, axis=0, keepdims=True)
                p_out[:, cs] = p.astype(BF16)
            if j_v is not None:
                acc_scr[:, cs] += jnp.dot(vt, p_in[:, cs], preferred_element_type=F32)

    def near_band(j):
        return jnp.logical_and(j >= nqb * i - 1, j <= nqb * i + nqb)

    def step_pair(j, last):
        def run(with_bias):
            step(j + 1, pb_scr, j, pa_scr, with_bias)
            step(None if last else j + 2, pa_scr, j + 1, pb_scr, with_bias)

        near = near_band(j + 1) if last else jnp.logical_or(near_band(j + 1), near_band(j + 2))
        lax.cond(near, lambda: run(True), lambda: run(False))

    @pl.when(fixed_ok)
    def _():
        m0 = hb + b_max
        mreg_scr[0] = m0 - c_left
        mreg_scr[1] = m0
        mreg_scr[2] = m0 - c_right

        def pair_body(jj, carry):
            step_pair(2 * jj, last=False)
            return carry

        step(0, pa_scr, None, None, True)
        lax.fori_loop(0, n_tiles // 2 - 1, pair_body, 0)
        step_pair(n_tiles - 2, last=True)

    def online_step(j):
        kt = k_ref[pl.ds(pl.multiple_of(j * t, t), t), :]
        vt = vt_ref[0, j]
        for c0 in range(0, 2 * tq, t):
            cs = slice(c0, c0 + t)
            tile_idx, region = block_relation(j, c0)
            c_far = jnp.where(region == 0, c_left, jnp.where(region == 2, c_right, 0.0))
            s = (jnp.dot(kt, q2_scr[:, cs], preferred_element_type=F32)
                 + (bias_ref[0, tile_idx] + c_far))
            m_old = m_scr[:, cs]
            m_new = jnp.maximum(m_old, jnp.max(s, axis=0, keepdims=True))
            p = jnp.exp2(s - m_new)
            alpha = jnp.exp2(m_old - m_new)
            l_scr[:, cs] = alpha * l_scr[:, cs] + jnp.sum(p, axis=0, keepdims=True)
            pv = jnp.dot(vt, p.astype(BF16), preferred_element_type=F32)
            acc_scr[:, cs] = alpha * acc_scr[:, cs] + pv
            m_scr[:, cs] = m_new

    @pl.when(jnp.logical_not(fixed_ok))
    def _():
        def body(j, carry):
            online_step(j)
            return carry

        m_scr[...] = jnp.full(m_scr.shape, NEG_INIT, F32)
        lax.fori_loop(0, n_tiles, body, 0)

    lq = lq_ref[...]
    lam = (jnp.exp(jnp.sum(lq[0:1] * lq[1:2], axis=-1, keepdims=True))
           - jnp.exp(jnp.sum(lq[2:3] * lq[3:4], axis=-1, keepdims=True)) + lambda_init)
    accn = acc_scr[...] * (1.0 / l_scr[...])
    o_t = accn[:, :tq] - lam * accn[:, tq:]
    o = o_t.T
    ms = jnp.mean(o * o, axis=-1, keepdims=True)
    y = ((o * lax.rsqrt(ms + EPS)) * gain_ref[...]) * (1.0 - lambda_init)
    o_ref[...] = (y * _silu(gate_ref[...].astype(F32))).astype(BF16)


def _attention(bnd, lq, qt4, proj, vt4, bias, gain, lambda_init):
    s = proj.shape[0]
    tq = TQ_ATT
    t = vt4.shape[-1]
    assert t == bias.shape[-1] and tq % t == 0
    n_tiles = s // t
    kblk = OFF_K // HEAD_DIM
    gblk = OFF_B_GATE // HEAD_DIM
    kern = functools.partial(_attn_kernel, lambda_init=lambda_init, n_tiles=n_tiles)
    return pl.pallas_call(
        kern,
        grid=(N_HEADS, s // tq),
        in_specs=[
            pl.BlockSpec(memory_space=pltpu.SMEM),
            pl.BlockSpec((4, HALF_DIM), lambda h, i: (0, 0)),
            pl.BlockSpec((1, tq // t, HEAD_DIM, t), lambda h, i: (h, i, 0, 0)),
            pl.BlockSpec((s, HEAD_DIM), lambda h, i: (0, kblk + h)),
            pl.BlockSpec((1, n_tiles, HEAD_DIM, t), lambda h, i: (h, 0, 0, 0)),
            pl.BlockSpec((1, N_BIAS_TILES, t, t), lambda h, i: (h, 0, 0, 0)),
            pl.BlockSpec((tq, HEAD_DIM), lambda h, i: (i, gblk + h)),
            pl.BlockSpec((1, HEAD_DIM), lambda h, i: (0, h)),
        ],
        out_specs=pl.BlockSpec((tq, HEAD_DIM), lambda h, i: (i, h)),
        out_shape=jax.ShapeDtypeStruct((s, D_DIFF), BF16),
        scratch_shapes=[
            pltpu.VMEM((HEAD_DIM, 2 * tq), BF16),
            pltpu.VMEM((1, 2 * tq), F32),
            pltpu.VMEM((1, 2 * tq), F32),
            pltpu.VMEM((HEAD_DIM, 2 * tq), F32),
            pltpu.VMEM((HEAD_DIM, 8), F32),
            pltpu.VMEM((3, 1, 2 * tq), F32),
            pltpu.VMEM((t, 2 * tq), BF16),
            pltpu.VMEM((t, 2 * tq), BF16),
        ],
        compiler_params=pltpu.CompilerParams(
            dimension_semantics=("arbitrary", "arbitrary"), vmem_limit_bytes=VMEM_LIMIT),
        name="diff_attention",
    )(bnd, lq, qt4, proj, vt4, bias, proj, gain)


def _outproj_kernel(ya_ref, ag_ref, yb_ref, u_ref, vsg_ref, cg_ref, x_ref, wout_ref, ws_ref,
                    bs_ref, vgain_ref, pgain_ref, o_ref, y_scr):
    tm = x_ref.shape[0]
    y_scr[:, 0:D_FOURIER] = (ya_ref[...] * _silu(ag_ref[...].astype(F32))).astype(BF16)
    y_scr[:, D_FOURIER:D_FOURIER + D_DIFF] = yb_ref[...]
    for g in range(D_GMLP // GROUP):
        cols = slice(g * GROUP, (g + 1) * GROUP)
        v = vsg_ref[:, cols].astype(F32)
        ms = jnp.mean(v * v, axis=-1, keepdims=True)
        vn = ((v * lax.rsqrt(ms + EPS)) * vgain_ref[:, cols]).astype(BF16)
        w = ws_ref[g]
        b = bs_ref[g]
        for c in range(tm // CHUNK):
            rows = slice(c * CHUNK, (c + 1) * CHUNK)
            mixed = jnp.dot(w, vn[rows], preferred_element_type=F32) + b
            gate = cg_ref[rows, cols].astype(F32)
            yc = (u_ref[rows, cols].astype(F32) * mixed) * _silu(gate)
            y_scr[rows, D_FOURIER + D_DIFF + g * GROUP:D_FOURIER + D_DIFF + (g + 1) * GROUP] = (
                yc.astype(BF16))
    y = jnp.dot(y_scr[...], wout_ref[...], preferred_element_type=F32)
    ms = jnp.mean(y * y, axis=-1, keepdims=True)
    o_ref[...] = x_ref[...] + (y * lax.rsqrt(ms + EPS)) * pgain_ref[...]


def _outproj(ya, yb, proj, x2, wout_bf, ws_bf, layer, bs3, vgain, pgain):
    s, d = x2.shape
    tm = TM_OUT
    ablk = OFF_A_GATE // D_FOURIER
    ublk, vblk, cblk = OFF_U // D_GMLP, OFF_VSG // D_GMLP, OFF_C_GATE // D_GMLP
    const = lambda shape: pl.BlockSpec(shape, lambda i: tuple(0 for _ in shape))
    return pl.pallas_call(
        _outproj_kernel,
        grid=(s // tm,),
        in_specs=[
            pl.BlockSpec((tm, D_FOURIER), lambda i: (i, 0)),
            pl.BlockSpec((tm, D_FOURIER), lambda i: (i, ablk)),
            pl.BlockSpec((tm, D_DIFF), lambda i: (i, 0)),
            pl.BlockSpec((tm, D_GMLP), lambda i: (i, ublk)),
            pl.BlockSpec((tm, D_GMLP), lambda i: (i, vblk)),
            pl.BlockSpec((tm, D_GMLP), lambda i: (i, cblk)),
            pl.BlockSpec((tm, d), lambda i: (i, 0)),
            pl.BlockSpec((None, d, d), lambda i: (layer, 0, 0)),
            pl.BlockSpec((None,) + ws_bf.shape[1:], lambda i: (layer, 0, 0, 0)),
            const(bs3.shape),
            const((1, D_GMLP)),
            const((1, d)),
        ],
        out_specs=pl.BlockSpec((tm, d), lambda i: (i, 0)),
        out_shape=jax.ShapeDtypeStruct((s, d), F32),
        scratch_shapes=[pltpu.VMEM((tm, d), BF16)],
        compiler_params=pltpu.CompilerParams(
            dimension_semantics=("arbitrary",), vmem_limit_bytes=VMEM_LIMIT),
        name="outproj",
    )(ya, proj, yb, proj, proj, proj, x2, wout_bf, ws_bf, bs3, vgain, pgain)


def kernel(x, w_in, pre_gain, post_gain, w_fourier, lambda_qk, diff_out_gain, sg_v_gain,
           w_spatial, b_spatial, w_out, rel_bias):
    b, s, d = x.shape
    depth = w_in.shape[0]
    assert b == 1 and w_in.shape[2] == D_IN
    x2 = x.reshape(s, d)

    q_scale = (HALF_DIM ** -0.5) * LOG2E
    colscale = jnp.ones((1, D_IN), F32).at[:, OFF_Q:OFF_Q + D_DIFF].set(q_scale)
    bias = _bias_tiles(rel_bias, T_ATT)
    bias_bounds = jnp.stack([jnp.max(rel_bias, axis=0), jnp.min(rel_bias, axis=0)], axis=1) * LOG2E
    consts = _dft_consts(s // FFT_N2, FFT_N2, GROUP)

    w_in_bf, w_out_bf = w_in.astype(BF16), w_out.astype(BF16)
    w_fourier_bf, w_spatial_bf = w_fourier.astype(BF16), w_spatial.astype(BF16)

    for l in range(depth):
        lambda_init = 0.8 - 0.6 * math.exp(-0.3 * l)
        proj, a_in, qt4, vt4 = _inproj(x2, pre_gain[l].reshape(1, d), w_in_bf, l, colscale)
        ya = _fourier_branch(a_in, w_fourier_bf, l, consts)
        yb = _attention(bias_bounds, lambda_qk[l], qt4, proj, vt4, bias,
                        diff_out_gain[l].reshape(1, D_DIFF), lambda_init)
        x2 = _outproj(ya, yb, proj, x2, w_out_bf, w_spatial_bf, l,
                      b_spatial[l].reshape(D_GMLP // GROUP, CHUNK, 1),
                      sg_v_gain[l].reshape(1, D_GMLP), post_gain[l].reshape(1, d))
    return x2.reshape(b, s, d)
```

```python
import functools
import math

import numpy as np
import jax
import jax.numpy as jnp
from jax import lax
from jax.experimental import pallas as pl
from jax.experimental.pallas import tpu as pltpu

F32 = jnp.float32
BF16 = jnp.bfloat16

EPS = 1e-6
LOG2E = math.log2(math.e)
NEG_INIT = -1e30
EXP2_SPAN = 120.0

D_FOURIER = 512
D_DIFF = 1024
D_GMLP = 512
GROUP = 128
HEAD_DIM = 128
HALF_DIM = 64
N_HEADS = 8
CHUNK = 128
N_BUCKETS = 32
MAX_DISTANCE = 128
OFF_A_IN, OFF_A_GATE, OFF_Q, OFF_K, OFF_V, OFF_B_GATE, OFF_U, OFF_VSG, OFF_C_GATE = (
    0, 512, 1024, 2048, 3072, 4096, 5120, 5632, 6144)
D_IN = 6656

MXU_COLS = 256
TM_IN = 512
TN_IN = 1664
TM_OUT = 512
T_ATT = 512
TQ_ATT = 2048
N_BIAS_TILES = 4
ZERO_BIAS_TILE = 3
FFT_N2 = 128
FFT_ROWS = 8
VMEM_LIMIT = 56 * 1024 * 1024
assert TM_IN % T_ATT == 0


def _silu(x):
    return x * (1.0 / (1.0 + jnp.exp(-x)))


def _inproj_kernel(x_ref, g_ref, w_ref, cs_ref, proj_ref, ain_ref, qt_ref, vt_ref, a_scr):
    j = pl.program_id(1)

    @pl.when(j == 0)
    def _():
        x = x_ref[...]
        ms = jnp.mean(x * x, axis=-1, keepdims=True)
        a_scr[...] = ((x * lax.rsqrt(ms + EPS)) * g_ref[...]).astype(BF16)

    acc = jnp.dot(a_scr[...], w_ref[...], preferred_element_type=F32)
    scaled = acc * cs_ref[...]
    proj_ref[...] = scaled.astype(BF16)

    @pl.when(j == 0)
    def _():
        ain_ref[...] = acc[:, :D_FOURIER]

    tn = proj_ref.shape[1]
    for jj in range(D_IN // tn):
        heads = [(dst, (c - off) // HEAD_DIM, c - jj * tn)
                 for dst, off in ((qt_ref, OFF_Q), (vt_ref, OFF_V))
                 for c in range(off, off + D_DIFF, HEAD_DIM)
                 if jj * tn <= c < (jj + 1) * tn]
        if heads:
            @pl.when(j == jj)
            def _(heads=heads):
                for dst, h, c0 in heads:
                    for r in range(dst.shape[1]):
                        rows = slice(r * T_ATT, (r + 1) * T_ATT)
                        dst[h, r] = scaled[rows, c0:c0 + HEAD_DIM].T.astype(BF16)


def _inproj(x2, gain, w_bf, layer, colscale):
    s, d = x2.shape
    n = w_bf.shape[2]
    grid = (s // TM_IN, n // TN_IN)
    head_tiles = pl.BlockSpec((N_HEADS, TM_IN // T_ATT, HEAD_DIM, T_ATT),
                              lambda i, j: (0, i, 0, 0))
    return pl.pallas_call(
        _inproj_kernel,
        grid=grid,
        in_specs=[
            pl.BlockSpec((TM_IN, d), lambda i, j: (i, 0)),
            pl.BlockSpec((1, d), lambda i, j: (0, 0)),
            pl.BlockSpec((None, d, TN_IN), lambda i, j: (layer, 0, j)),
            pl.BlockSpec((1, TN_IN), lambda i, j: (0, j)),
        ],
        out_specs=[
            pl.BlockSpec((TM_IN, TN_IN), lambda i, j: (i, j)),
            pl.BlockSpec((TM_IN, D_FOURIER), lambda i, j: (i, 0)),
            head_tiles,
            head_tiles,
        ],
        out_shape=[
            jax.ShapeDtypeStruct((s, n), BF16),
            jax.ShapeDtypeStruct((s, D_FOURIER), F32),
            jax.ShapeDtypeStruct((N_HEADS, s // T_ATT, HEAD_DIM, T_ATT), BF16),
            jax.ShapeDtypeStruct((N_HEADS, s // T_ATT, HEAD_DIM, T_ATT), BF16),
        ],
        scratch_shapes=[pltpu.VMEM((TM_IN, d), BF16)],
        compiler_params=pltpu.CompilerParams(
            dimension_semantics=("arbitrary", "arbitrary"),
            vmem_limit_bytes=VMEM_LIMIT),
        name="inproj",
    )(x2, gain, w_bf, colscale)


def _dft_consts(n1, n2, c):
    def cs(n):
        idx = np.arange(n)
        ang = 2.0 * np.pi * ((idx[:, None] * idx[None, :]) % n) / n
        return np.cos(ang), np.sin(ang)

    c1, s1 = cs(n1)
    c2, s2 = cs(n2)
    cc, sc = cs(c)
    f1 = np.kron(np.concatenate([c1, -s1], axis=0), np.eye(FFT_ROWS))
    f2 = np.block([[c2, s2], [s2, -c2]])
    n = n1 * n2
    k1 = np.arange(n1)[:, None]
    m2 = np.arange(n2)[None, :]
    ang = 2.0 * np.pi * ((k1 * m2) % n) / n
    tw_c = np.cos(ang)[:, :, None]
    tw_s = np.sin(ang)[:, :, None]
    ch = np.concatenate([cc, -sc], axis=0)
    f32 = lambda a: jnp.asarray(a.astype(np.float32))
    return f32(f1), f32(f2), f32(tw_c), f32(tw_s), f32(ch)


def _fft1_kernel(f1_ref, x_ref, y_ref):
    n1, rows, c = x_ref.shape
    x = x_ref[...].reshape(n1 * rows, c).astype(BF16)
    y = jnp.dot(f1_ref[...].astype(BF16), x, preferred_element_type=F32)
    y_ref[...] = y.reshape(y_ref.shape)


def _fft1(f1, x3):
    n1, n2, c = x3.shape
    return pl.pallas_call(
        _fft1_kernel,
        grid=(n2 // FFT_ROWS,),
        in_specs=[pl.BlockSpec(f1.shape, lambda j: (0, 0)),
                  pl.BlockSpec((n1, FFT_ROWS, c), lambda j: (0, j, 0))],
        out_specs=pl.BlockSpec((2 * n1, FFT_ROWS, c), lambda j: (0, j, 0)),
        out_shape=jax.ShapeDtypeStruct((2 * n1, n2, c), F32),
        compiler_params=pltpu.CompilerParams(
            dimension_semantics=("arbitrary",), vmem_limit_bytes=VMEM_LIMIT),
        name="fft_stage1",
    )(f1, x3)


def _fft2_kernel(y_ref, twc_ref, tws_ref, f2_ref, ch_ref, wf_ref, o_ref, *, scale):
    f2 = f2_ref[...].astype(BF16)
    ch = ch_ref[...].astype(BF16)
    _, rows, n2, c = y_ref.shape
    ps, qs = [], []
    for r in range(rows):
        yr = y_ref[0, r]
        yi = y_ref[1, r]
        tc = twc_ref[r]
        ts = tws_ref[r]
        zr = yr * tc + yi * ts
        zi = yi * tc - yr * ts
        z = jnp.concatenate([zr, zi], axis=0).astype(BF16)
        pq = jnp.dot(f2, z, preferred_element_type=F32)
        ps.append(pq[:n2].astype(BF16))
        qs.append(pq[n2:].astype(BF16))
    p = jnp.concatenate(ps, axis=0)
    q = jnp.concatenate(qs, axis=0)
    fs = []
    for g in range(c // GROUP):
        cols = slice(g * GROUP, (g + 1) * GROUP)
        pq_g = jnp.concatenate([p[:, cols], q[:, cols]], axis=1)
        fs.append(jnp.dot(pq_g, ch, preferred_element_type=F32))
    f = (jnp.concatenate(fs, axis=1) * scale).astype(BF16)
    ya = jnp.dot(f, wf_ref[...], preferred_element_type=F32)
    for r in range(rows):
        o_ref[:, r, :] = ya[r * n2:(r + 1) * n2]


def _fft2(y4, twc, tws, f2, ch, wf_bf, layer, scale):
    _, n1, n2, c = y4.shape
    const = lambda shape: pl.BlockSpec(shape, lambda k: tuple(0 for _ in shape))
    return pl.pallas_call(
        functools.partial(_fft2_kernel, scale=scale),
        grid=(n1 // FFT_ROWS,),
        in_specs=[
            pl.BlockSpec((2, FFT_ROWS, n2, c), lambda k: (0, k, 0, 0)),
            pl.BlockSpec((FFT_ROWS, n2, 1), lambda k: (k, 0, 0)),
            pl.BlockSpec((FFT_ROWS, n2, 1), lambda k: (k, 0, 0)),
            const((2 * n2, 2 * n2)),
            const(ch.shape),
            pl.BlockSpec((None, c, c), lambda k: (layer, 0, 0)),
        ],
        out_specs=pl.BlockSpec((n2, FFT_ROWS, c), lambda k: (0, k, 0)),
        out_shape=jax.ShapeDtypeStruct((n2, n1, c), F32),
        compiler_params=pltpu.CompilerParams(
            dimension_semantics=("arbitrary",), vmem_limit_bytes=VMEM_LIMIT),
        name="fft_stage2",
    )(y4, twc, tws, f2, ch, wf_bf)


def _fourier_branch(a_in, wf_bf, layer, consts):
    s, c = a_in.shape
    n2 = FFT_N2
    n1 = s // n2
    f1, f2, twc, tws, ch = consts
    y = _fft1(f1, a_in.reshape(n1, n2, c))
    scale = 1.0 / math.sqrt(s * GROUP)
    out = _fft2(y.reshape(2, n1, n2, c), twc, tws, f2, ch, wf_bf, layer, scale)
    return out.reshape(s, c)


def _t5_bucket(rel):
    nb = N_BUCKETS // 2
    ret = (rel > 0).astype(jnp.int32) * nb
    n = jnp.abs(rel)
    max_exact = nb // 2
    nf = jnp.maximum(n, 1).astype(jnp.float32)
    large = max_exact + (jnp.log(nf / max_exact) / math.log(MAX_DISTANCE / max_exact)
                         * (nb - max_exact)).astype(jnp.int32)
    large = jnp.minimum(large, nb - 1)
    return ret + jnp.where(n < max_exact, n, large)


def _bias_kernel(rb_ref, bk_ref, o_ref):
    h = pl.program_id(0)
    half = N_BUCKETS // 2
    t = bk_ref.shape[1]
    blk = MAX_DISTANCE
    for tile, d in enumerate((-1, 0, 1)):
        for kb in range(t // blk):
            for qb in range(t // blk):
                rel_lo = d * t + kb * blk - (qb * blk + blk - 1)
                rel_hi = d * t + kb * blk + blk - 1 - qb * blk
                rows, cols = slice(kb * blk, (kb + 1) * blk), slice(qb * blk, (qb + 1) * blk)
                if rel_hi <= -blk:
                    val = jnp.full((blk, blk), rb_ref[half - 1, h], F32)
                elif rel_lo >= blk:
                    val = jnp.full((blk, blk), rb_ref[N_BUCKETS - 1, h], F32)
                else:
                    bk = bk_ref[tile, rows, cols]
                    val = jnp.zeros(bk.shape, F32)
                    for b in range(0 if rel_lo <= 0 else half, half if rel_hi < 0 else N_BUCKETS):
                        val = jnp.where(bk == b, rb_ref[b, h], val)
                o_ref[0, tile, rows, cols] = val * LOG2E
    o_ref[0, ZERO_BIAS_TILE] = jnp.zeros((t, t), F32)


def _bias_tiles(rel_bias, t):
    kk = jnp.arange(t, dtype=jnp.int32)[:, None]
    rr = jnp.arange(t, dtype=jnp.int32)[None, :]
    rel = jnp.stack([d * t + kk - rr for d in (-1, 0, 1)])
    bucket = _t5_bucket(rel)
    return pl.pallas_call(
        _bias_kernel,
        grid=(N_HEADS,),
        in_specs=[pl.BlockSpec(memory_space=pltpu.SMEM),
                  pl.BlockSpec(bucket.shape, lambda h: (0, 0, 0))],
        out_specs=pl.BlockSpec((1, N_BIAS_TILES, t, t), lambda h: (h, 0, 0, 0)),
        out_shape=jax.ShapeDtypeStruct((N_HEADS, N_BIAS_TILES, t, t), F32),
        compiler_params=pltpu.CompilerParams(
            dimension_semantics=("arbitrary",), vmem_limit_bytes=VMEM_LIMIT),
        name="bias_tiles",
    )(rel_bias, bucket)


def _attn_kernel(bnd_ref, lq_ref, qt_ref, k_ref, vt_ref, bias_ref, gate_ref, gain_ref, o_ref,
                 q2_scr, m_scr, l_scr, acc_scr, kabs_scr, mreg_scr, pa_scr, pb_scr,
                 *, lambda_init, n_tiles):
    tq = qt_ref.shape[1] * qt_ref.shape[3]
    t = vt_ref.shape[-1]
    nqb = tq // t
    h = pl.program_id(0)
    i = pl.program_id(1)

    @pl.when(i == 0)
    def _():
        ka = jnp.max(jnp.abs(k_ref[...].astype(F32)), axis=0, keepdims=True)
        kabs_scr[...] = jnp.broadcast_to(ka, (8, HEAD_DIM)).T

    qt = jnp.concatenate([qt_ref[0, r] for r in range(qt_ref.shape[1])], axis=1)
    row = lax.broadcasted_iota(jnp.int32, qt.shape, 0)
    zero = jnp.zeros_like(qt)
    q2_scr[:, :tq] = jnp.where(row < HALF_DIM, qt, zero)
    q2_scr[:, tq:] = jnp.where(row >= HALF_DIM, qt, zero)
    l_scr[...] = jnp.zeros(l_scr.shape, F32)
    acc_scr[...] = jnp.zeros(acc_scr.shape, F32)

    c_left = bias_ref[0, 0, 0:1, t - 1:t]
    c_right = bias_ref[0, 2, t - 1:t, 0:1]

    wq = jnp.abs(qt.astype(F32)) * kabs_scr[:, 0:1]
    hb = jnp.concatenate([jnp.sum(wq[:HALF_DIM], axis=0, keepdims=True),
                          jnp.sum(wq[HALF_DIM:], axis=0, keepdims=True)], axis=1)
    b_max = bnd_ref[h, 0]
    b_min = bnd_ref[h, 1]
    fixed_ok = 2.0 * jnp.max(hb) + (b_max - b_min) <= EXP2_SPAN

    def block_relation(j, col0):
        e = j - (nqb * i + (col0 % tq) // t)
        tile_idx = jnp.where(jnp.abs(e) <= 1, e + 1, ZERO_BIAS_TILE)
        region = jnp.where(e < -1, 0, jnp.where(e > 1, 2, 1))
        return tile_idx, region

    def step(j_p, p_out, j_v, p_in, with_bias):
        if j_p is not None:
            kt = k_ref[pl.ds(pl.multiple_of(j_p * t, t), t), :]
        if j_v is not None:
            vt = vt_ref[0, j_v]
        for c0 in range(0, 2 * tq, MXU_COLS):
            cs = slice(c0, c0 + MXU_COLS)
            if j_p is not None:
                tile_idx, region = block_relation(j_p, c0)
                s = jnp.dot(kt, q2_scr[:, cs], preferred_element_type=F32)
                if with_bias:
                    b0 = c0 % t
                    s = s + bias_ref[0, tile_idx, :, b0:b0 + MXU_COLS]
                p = jnp.exp2(s - mreg_scr[region, :, cs])
                l_scr[:, cs] += jnp.sum(p, axis=0, keepdims=True)
                p_out[:, cs] = p.astype(BF16)
            if j_v is not None:
                acc_scr[:, cs] += jnp.dot(vt, p_in[:, cs], preferred_element_type=F32)

    def near_band(j):
        return jnp.logical_and(j >= nqb * i - 1, j <= nqb * i + nqb)

    def step_pair(j, last):
        def run(with_bias):
            step(j + 1, pb_scr, j, pa_scr, with_bias)
            step(None if last else j + 2, pa_scr, j + 1, pb_scr, with_bias)

        near = near_band(j + 1) if last else jnp.logical_or(near_band(j + 1), near_band(j + 2))
        lax.cond(near, lambda: run(True), lambda: run(False))

    @pl.when(fixed_ok)
    def _():
        m0 = hb + b_max
        mreg_scr[0] = m0 - c_left
        mreg_scr[1] = m0
        mreg_scr[2] = m0 - c_right

        def pair_body(jj, carry):
            step_pair(2 * jj, last=False)
            return carry

        step(0, pa_scr, None, None, True)
        lax.fori_loop(0, n_tiles // 2 - 1, pair_body, 0)
        step_pair(n_tiles - 2, last=True)

    def online_step(j):
        kt = k_ref[pl.ds(pl.multiple_of(j * t, t), t), :]
        vt = vt_ref[0, j]
        for c0 in range(0, 2 * tq, t):
            cs = slice(c0, c0 + t)
            tile_idx, region = block_relation(j, c0)
            c_far = jnp.where(region == 0, c_left, jnp.where(region == 2, c_right, 0.0))
            s = (jnp.dot(kt, q2_scr[:, cs], preferred_element_type=F32)
                 + (bias_ref[0, tile_idx] + c_far))
            m_old = m_scr[:, cs]
            m_new = jnp.maximum(m_old, jnp.max(s, axis=0, keepdims=True))
            p = jnp.exp2(s - m_new)
            alpha = jnp.exp2(m_old - m_new)
            l_scr[:, cs] = alpha * l_scr[:, cs] + jnp.sum(p, axis=0, keepdims=True)
            pv = jnp.dot(vt, p.astype(BF16), preferred_element_type=F32)
            acc_scr[:, cs] = alpha * acc_scr[:, cs] + pv
            m_scr[:, cs] = m_new

    @pl.when(jnp.logical_not(fixed_ok))
    def _():
        def body(j, carry):
            online_step(j)
            return carry

        m_scr[...] = jnp.full(m_scr.shape, NEG_INIT, F32)
        lax.fori_loop(0, n_tiles, body, 0)

    lq = lq_ref[...]
    lam = (jnp.exp(jnp.sum(lq[0:1] * lq[1:2], axis=-1, keepdims=True))
           - jnp.exp(jnp.sum(lq[2:3] * lq[3:4], axis=-1, keepdims=True)) + lambda_init)
    accn = acc_scr[...] * (1.0 / l_scr[...])
    o_t = accn[:, :tq] - lam * accn[:, tq:]
    o = o_t.T
    ms = jnp.mean(o * o, axis=-1, keepdims=True)
    y = ((o * lax.rsqrt(ms + EPS)) * gain_ref[...]) * (1.0 - lambda_init)
    o_ref[...] = (y * _silu(gate_ref[...].astype(F32))).astype(BF16)


def _attention(bnd, lq, qt4, proj, vt4, bias, gain, lambda_init):
    s = proj.shape[0]
    tq = TQ_ATT
    t = vt4.shape[-1]
    assert t == bias.shape[-1] and tq % t == 0
    n_tiles = s // t
    kblk = OFF_K // HEAD_DIM
    gblk = OFF_B_GATE // HEAD_DIM
    kern = functools.partial(_attn_kernel, lambda_init=lambda_init, n_tiles=n_tiles)
    return pl.pallas_call(
        kern,
        grid=(N_HEADS, s // tq),
        in_specs=[
            pl.BlockSpec(memory_space=pltpu.SMEM),
            pl.BlockSpec((4, HALF_DIM), lambda h, i: (0, 0)),
            pl.BlockSpec((1, tq // t, HEAD_DIM, t), lambda h, i: (h, i, 0, 0)),
            pl.BlockSpec((s, HEAD_DIM), lambda h, i: (0, kblk + h)),
            pl.BlockSpec((1, n_tiles, HEAD_DIM, t), lambda h, i: (h, 0, 0, 0)),
            pl.BlockSpec((1, N_BIAS_TILES, t, t), lambda h, i: (h, 0, 0, 0)),
            pl.BlockSpec((tq, HEAD_DIM), lambda h, i: (i, gblk + h)),
            pl.BlockSpec((1, HEAD_DIM), lambda h, i: (0, h)),
        ],
        out_specs=pl.BlockSpec((tq, HEAD_DIM), lambda h, i: (i, h)),
        out_shape=jax.ShapeDtypeStruct((s, D_DIFF), BF16),
        scratch_shapes=[
            pltpu.VMEM((HEAD_DIM, 2 * tq), BF16),
            pltpu.VMEM((1, 2 * tq), F32),
            pltpu.VMEM((1, 2 * tq), F32),
            pltpu.VMEM((HEAD_DIM, 2 * tq), F32),
            pltpu.VMEM((HEAD_DIM, 8), F32),
            pltpu.VMEM((3, 1, 2 * tq), F32),
            pltpu.VMEM((t, 2 * tq), BF16),
            pltpu.VMEM((t, 2 * tq), BF16),
        ],
        compiler_params=pltpu.CompilerParams(
            dimension_semantics=("arbitrary", "arbitrary"), vmem_limit_bytes=VMEM_LIMIT),
        name="diff_attention",
    )(bnd, lq, qt4, proj, vt4, bias, proj, gain)


def _outproj_kernel(ya_ref, ag_ref, yb_ref, u_ref, vsg_ref, cg_ref, x_ref, wout_ref, ws_ref,
                    bs_ref, vgain_ref, pgain_ref, o_ref, y_scr):
    tm = x_ref.shape[0]
    y_scr[:, 0:D_FOURIER] = (ya_ref[...] * _silu(ag_ref[...].astype(F32))).astype(BF16)
    y_scr[:, D_FOURIER:D_FOURIER + D_DIFF] = yb_ref[...]
    for g in range(D_GMLP // GROUP):
        cols = slice(g * GROUP, (g + 1) * GROUP)
        v = vsg_ref[:, cols].astype(F32)
        ms = jnp.mean(v * v, axis=-1, keepdims=True)
        vn = ((v * lax.rsqrt(ms + EPS)) * vgain_ref[:, cols]).astype(BF16)
        w = ws_ref[g]
        b = bs_ref[g]
        for c in range(tm // CHUNK):
            rows = slice(c * CHUNK, (c + 1) * CHUNK)
            mixed = jnp.dot(w, vn[rows], preferred_element_type=F32) + b
            gate = cg_ref[rows, cols].astype(F32)
            yc = (u_ref[rows, cols].astype(F32) * mixed) * _silu(gate)
            y_scr[rows, D_FOURIER + D_DIFF + g * GROUP:D_FOURIER + D_DIFF + (g + 1) * GROUP] = (
                yc.astype(BF16))
    y = jnp.dot(y_scr[...], wout_ref[...], preferred_element_type=F32)
    ms = jnp.mean(y * y, axis=-1, keepdims=True)
    o_ref[...] = x_ref[...] + (y * lax.rsqrt(ms + EPS)) * pgain_ref[...]


def _outproj(ya, yb, proj, x2, wout_bf, ws_bf, layer, bs3, vgain, pgain):
    s, d = x2.shape
    tm = TM_OUT
    ablk = OFF_A_GATE // D_FOURIER
    ublk, vblk, cblk = OFF_U // D_GMLP, OFF_VSG // D_GMLP, OFF_C_GATE // D_GMLP
    const = lambda shape: pl.BlockSpec(shape, lambda i: tuple(0 for _ in shape))
    return pl.pallas_call(
        _outproj_kernel,
        grid=(s // tm,),
        in_specs=[
            pl.BlockSpec((tm, D_FOURIER), lambda i: (i, 0)),
            pl.BlockSpec((tm, D_FOURIER), lambda i: (i, ablk)),
            pl.BlockSpec((tm, D_DIFF), lambda i: (i, 0)),
            pl.BlockSpec((tm, D_GMLP), lambda i: (i, ublk)),
            pl.BlockSpec((tm, D_GMLP), lambda i: (i, vblk)),
            pl.BlockSpec((tm, D_GMLP), lambda i: (i, cblk)),
            pl.BlockSpec((tm, d), lambda i: (i, 0)),
            pl.BlockSpec((None, d, d), lambda i: (layer, 0, 0)),
            pl.BlockSpec((None,) + ws_bf.shape[1:], lambda i: (layer, 0, 0, 0)),
            const(bs3.shape),
            const((1, D_GMLP)),
            const((1, d)),
        ],
        out_specs=pl.BlockSpec((tm, d), lambda i: (i, 0)),
        out_shape=jax.ShapeDtypeStruct((s, d), F32),
        scratch_shapes=[pltpu.VMEM((tm, d), BF16)],
        compiler_params=pltpu.CompilerParams(
            dimension_semantics=("arbitrary",), vmem_limit_bytes=VMEM_LIMIT),
        name="outproj",
    )(ya, proj, yb, proj, proj, proj, x2, wout_bf, ws_bf, bs3, vgain, pgain)


def kernel(x, w_in, pre_gain, post_gain, w_fourier, lambda_qk, diff_out_gain, sg_v_gain,
           w_spatial, b_spatial, w_out, rel_bias):
    b, s, d = x.shape
    depth = w_in.shape[0]
    assert b == 1 and w_in.shape[2] == D_IN
    x2 = x.reshape(s, d)

    q_scale = (HALF_DIM ** -0.5) * LOG2E
    colscale = jnp.ones((1, D_IN), F32).at[:, OFF_Q:OFF_Q + D_DIFF].set(q_scale)
    bias = _bias_tiles(rel_bias, T_ATT)
    bias_bounds = jnp.stack([jnp.max(rel_bias, axis=0), jnp.min(rel_bias, axis=0)], axis=1) * LOG2E
    consts = _dft_consts(s // FFT_N2, FFT_N2, GROUP)

    w_in_bf, w_out_bf = w_in.astype(BF16), w_out.astype(BF16)
    w_fourier_bf, w_spatial_bf = w_fourier.astype(BF16), w_spatial.astype(BF16)

    for l in range(depth):
        lambda_init = 0.8 - 0.6 * math.exp(-0.3 * l)
        proj, a_in, qt4, vt4 = _inproj(x2, pre_gain[l].reshape(1, d), w_in_bf, l, colscale)
        ya = _fourier_branch(a_in, w_fourier_bf, l, consts)
        yb = _attention(bias_bounds, lambda_qk[l], qt4, proj, vt4, bias,
                        diff_out_gain[l].reshape(1, D_DIFF), lambda_init)
        x2 = _outproj(ya, yb, proj, x2, w_out_bf, w_spatial_bf, l,
                      b_spatial[l].reshape(D_GMLP // GROUP, CHUNK, 1),
                      sg_v_gain[l].reshape(1, D_GMLP), post_gain[l].reshape(1, d))
    return x2.reshape(b, s, d)
```

```python
import functools
import math

import numpy as np
import jax
import jax.numpy as jnp
from jax import lax
from jax.experimental import pallas as pl
from jax.experimental.pallas import tpu as pltpu

F32 = jnp.float32
BF16 = jnp.bfloat16

EPS = 1e-6
LOG2E = math.log2(math.e)
NEG_INIT = -1e30
EXP2_SPAN = 120.0

D_FOURIER = 512
D_DIFF = 1024
D_GMLP = 512
GROUP = 128
HEAD_DIM = 128
HALF_DIM = 64
N_HEADS = 8
CHUNK = 128
N_BUCKETS = 32
MAX_DISTANCE = 128
OFF_A_IN, OFF_A_GATE, OFF_Q, OFF_K, OFF_V, OFF_B_GATE, OFF_U, OFF_VSG, OFF_C_GATE = (
    0, 512, 1024, 2048, 3072, 4096, 5120, 5632, 6144)
D_IN = 6656

MXU_COLS = 256
TM_IN = 512
TN_IN = 3328
TM_OUT = 512
T_ATT = 512
TQ_ATT = 2048
N_BIAS_TILES = 4
ZERO_BIAS_TILE = 3
FFT_N2 = 128
FFT_ROWS = 8
VMEM_LIMIT = 56 * 1024 * 1024
VMEM_LIMIT_INPROJ = 62 * 1024 * 1024
assert TM_IN % T_ATT == 0


def _silu(x):
    return x * (1.0 / (1.0 + jnp.exp(-x)))


def _inproj_kernel(x_ref, g_ref, w_ref, cs_ref, proj_ref, ain_ref, qt_ref, vt_ref, a_scr):
    j = pl.program_id(1)

    @pl.when(j == 0)
    def _():
        x = x_ref[...]
        ms = jnp.mean(x * x, axis=-1, keepdims=True)
        a_scr[...] = ((x * lax.rsqrt(ms + EPS)) * g_ref[...]).astype(BF16)

    acc = jnp.dot(a_scr[...], w_ref[...], preferred_element_type=F32)
    scaled = acc * cs_ref[...]
    proj_ref[...] = scaled.astype(BF16)

    @pl.when(j == 0)
    def _():
        ain_ref[...] = acc[:, :D_FOURIER]

    tn = proj_ref.shape[1]
    for jj in range(D_IN // tn):
        heads = [(dst, (c - off) // HEAD_DIM, c - jj * tn)
                 for dst, off in ((qt_ref, OFF_Q), (vt_ref, OFF_V))
                 for c in range(off, off + D_DIFF, HEAD_DIM)
                 if jj * tn <= c < (jj + 1) * tn]
        if heads:
            @pl.when(j == jj)
            def _(heads=heads):
                for dst, h, c0 in heads:
                    for r in range(dst.shape[1]):
                        rows = slice(r * T_ATT, (r + 1) * T_ATT)
                        dst[h, r] = scaled[rows, c0:c0 + HEAD_DIM].T.astype(BF16)


def _inproj(x2, gain, w_bf, layer, colscale):
    s, d = x2.shape
    n = w_bf.shape[2]
    grid = (s // TM_IN, n // TN_IN)
    head_tiles = pl.BlockSpec((N_HEADS, TM_IN // T_ATT, HEAD_DIM, T_ATT),
                              lambda i, j: (0, i, 0, 0))
    return pl.pallas_call(
        _inproj_kernel,
        grid=grid,
        in_specs=[
            pl.BlockSpec((TM_IN, d), lambda i, j: (i, 0)),
            pl.BlockSpec((1, d), lambda i, j: (0, 0)),
            pl.BlockSpec((None, d, TN_IN), lambda i, j: (layer, 0, j)),
            pl.BlockSpec((1, TN_IN), lambda i, j: (0, j)),
        ],
        out_specs=[
            pl.BlockSpec((TM_IN, TN_IN), lambda i, j: (i, j)),
            pl.BlockSpec((TM_IN, D_FOURIER), lambda i, j: (i, 0)),
            head_tiles,
            head_tiles,
        ],
        out_shape=[
            jax.ShapeDtypeStruct((s, n), BF16),
            jax.ShapeDtypeStruct((s, D_FOURIER), F32),
            jax.ShapeDtypeStruct((N_HEADS, s // T_ATT, HEAD_DIM, T_ATT), BF16),
            jax.ShapeDtypeStruct((N_HEADS, s // T_ATT, HEAD_DIM, T_ATT), BF16),
        ],
        scratch_shapes=[pltpu.VMEM((TM_IN, d), BF16)],
        compiler_params=pltpu.CompilerParams(
            dimension_semantics=("arbitrary", "arbitrary"),
            vmem_limit_bytes=VMEM_LIMIT_INPROJ),
        name="inproj",
    )(x2, gain, w_bf, colscale)


def _dft_consts(n1, n2, c):
    def cs(n):
        idx = np.arange(n)
        ang = 2.0 * np.pi * ((idx[:, None] * idx[None, :]) % n) / n
        return np.cos(ang), np.sin(ang)

    c1, s1 = cs(n1)
    c2, s2 = cs(n2)
    cc, sc = cs(c)
    f1 = np.kron(np.concatenate([c1, -s1], axis=0), np.eye(FFT_ROWS))
    f2 = np.block([[c2, s2], [s2, -c2]])
    n = n1 * n2
    k1 = np.arange(n1)[:, None]
    m2 = np.arange(n2)[None, :]
    ang = 2.0 * np.pi * ((k1 * m2) % n) / n
    tw_c = np.cos(ang)[:, :, None]
    tw_s = np.sin(ang)[:, :, None]
    ch = np.concatenate([cc, -sc], axis=0)
    f32 = lambda a: jnp.asarray(a.astype(np.float32))
    return f32(f1), f32(f2), f32(tw_c), f32(tw_s), f32(ch)


def _fft1_kernel(f1_ref, x_ref, y_ref):
    n1, rows, c = x_ref.shape
    x = x_ref[...].reshape(n1 * rows, c).astype(BF16)
    y = jnp.dot(f1_ref[...].astype(BF16), x, preferred_element_type=F32)
    y_ref[...] = y.reshape(y_ref.shape)


def _fft1(f1, x3):
    n1, n2, c = x3.shape
    return pl.pallas_call(
        _fft1_kernel,
        grid=(n2 // FFT_ROWS,),
        in_specs=[pl.BlockSpec(f1.shape, lambda j: (0, 0)),
                  pl.BlockSpec((n1, FFT_ROWS, c), lambda j: (0, j, 0))],
        out_specs=pl.BlockSpec((2 * n1, FFT_ROWS, c), lambda j: (0, j, 0)),
        out_shape=jax.ShapeDtypeStruct((2 * n1, n2, c), F32),
        compiler_params=pltpu.CompilerParams(
            dimension_semantics=("arbitrary",), vmem_limit_bytes=VMEM_LIMIT),
        name="fft_stage1",
    )(f1, x3)


def _fft2_kernel(y_ref, twc_ref, tws_ref, f2_ref, ch_ref, wf_ref, o_ref, *, scale):
    f2 = f2_ref[...].astype(BF16)
    ch = ch_ref[...].astype(BF16)
    _, rows, n2, c = y_ref.shape
    ps, qs = [], []
    for r in range(rows):
        yr = y_ref[0, r]
        yi = y_ref[1, r]
        tc = twc_ref[r]
        ts = tws_ref[r]
        zr = yr * tc + yi * ts
        zi = yi * tc - yr * ts
        z = jnp.concatenate([zr, zi], axis=0).astype(BF16)
        pq = jnp.dot(f2, z, preferred_element_type=F32)
        ps.append(pq[:n2].astype(BF16))
        qs.append(pq[n2:].astype(BF16))
    p = jnp.concatenate(ps, axis=0)
    q = jnp.concatenate(qs, axis=0)
    fs = []
    for g in range(c // GROUP):
        cols = slice(g * GROUP, (g + 1) * GROUP)
        pq_g = jnp.concatenate([p[:, cols], q[:, cols]], axis=1)
        fs.append(jnp.dot(pq_g, ch, preferred_element_type=F32))
    f = (jnp.concatenate(fs, axis=1) * scale).astype(BF16)
    ya = jnp.dot(f, wf_ref[...], preferred_element_type=F32)
    for r in range(rows):
        o_ref[:, r, :] = ya[r * n2:(r + 1) * n2]


def _fft2(y4, twc, tws, f2, ch, wf_bf, layer, scale):
    _, n1, n2, c = y4.shape
    const = lambda shape: pl.BlockSpec(shape, lambda k: tuple(0 for _ in shape))
    return pl.pallas_call(
        functools.partial(_fft2_kernel, scale=scale),
        grid=(n1 // FFT_ROWS,),
        in_specs=[
            pl.BlockSpec((2, FFT_ROWS, n2, c), lambda k: (0, k, 0, 0)),
            pl.BlockSpec((FFT_ROWS, n2, 1), lambda k: (k, 0, 0)),
            pl.BlockSpec((FFT_ROWS, n2, 1), lambda k: (k, 0, 0)),
            const((2 * n2, 2 * n2)),
            const(ch.shape),
            pl.BlockSpec((None, c, c), lambda k: (layer, 0, 0)),
        ],
        out_specs=pl.BlockSpec((n2, FFT_ROWS, c), lambda k: (0, k, 0)),
        out_shape=jax.ShapeDtypeStruct((n2, n1, c), F32),
        compiler_params=pltpu.CompilerParams(
            dimension_semantics=("arbitrary",), vmem_limit_bytes=VMEM_LIMIT),
        name="fft_stage2",
    )(y4, twc, tws, f2, ch, wf_bf)


def _fourier_branch(a_in, wf_bf, layer, consts):
    s, c = a_in.shape
    n2 = FFT_N2
    n1 = s // n2
    f1, f2, twc, tws, ch = consts
    y = _fft1(f1, a_in.reshape(n1, n2, c))
    scale = 1.0 / math.sqrt(s * GROUP)
    out = _fft2(y.reshape(2, n1, n2, c), twc, tws, f2, ch, wf_bf, layer, scale)
    return out.reshape(s, c)


def _t5_bucket(rel):
    nb = N_BUCKETS // 2
    ret = (rel > 0).astype(jnp.int32) * nb
    n = jnp.abs(rel)
    max_exact = nb // 2
    nf = jnp.maximum(n, 1).astype(jnp.float32)
    large = max_exact + (jnp.log(nf / max_exact) / math.log(MAX_DISTANCE / max_exact)
                         * (nb - max_exact)).astype(jnp.int32)
    large = jnp.minimum(large, nb - 1)
    return ret + jnp.where(n < max_exact, n, large)


def _bias_kernel(rb_ref, bk_ref, o_ref):
    h = pl.program_id(0)
    half = N_BUCKETS // 2
    t = bk_ref.shape[1]
    blk = MAX_DISTANCE
    for tile, d in enumerate((-1, 0, 1)):
        for kb in range(t // blk):
            for qb in range(t // blk):
                rel_lo = d * t + kb * blk - (qb * blk + blk - 1)
                rel_hi = d * t + kb * blk + blk - 1 - qb * blk
                rows, cols = slice(kb * blk, (kb + 1) * blk), slice(qb * blk, (qb + 1) * blk)
                if rel_hi <= -blk:
                    val = jnp.full((blk, blk), rb_ref[half - 1, h], F32)
                elif rel_lo >= blk:
                    val = jnp.full((blk, blk), rb_ref[N_BUCKETS - 1, h], F32)
                else:
                    bk = bk_ref[tile, rows, cols]
                    val = jnp.zeros(bk.shape, F32)
                    for b in range(0 if rel_lo <= 0 else half, half if rel_hi < 0 else N_BUCKETS):
                        val = jnp.where(bk == b, rb_ref[b, h], val)
                o_ref[0, tile, rows, cols] = val * LOG2E
    o_ref[0, ZERO_BIAS_TILE] = jnp.zeros((t, t), F32)


def _bias_tiles(rel_bias, t):
    kk = jnp.arange(t, dtype=jnp.int32)[:, None]
    rr = jnp.arange(t, dtype=jnp.int32)[None, :]
    rel = jnp.stack([d * t + kk - rr for d in (-1, 0, 1)])
    bucket = _t5_bucket(rel)
    return pl.pallas_call(
        _bias_kernel,
        grid=(N_HEADS,),
        in_specs=[pl.BlockSpec(memory_space=pltpu.SMEM),
                  pl.BlockSpec(bucket.shape, lambda h: (0, 0, 0))],
        out_specs=pl.BlockSpec((1, N_BIAS_TILES, t, t), lambda h: (h, 0, 0, 0)),
        out_shape=jax.ShapeDtypeStruct((N_HEADS, N_BIAS_TILES, t, t), F32),
        compiler_params=pltpu.CompilerParams(
            dimension_semantics=("arbitrary",), vmem_limit_bytes=VMEM_LIMIT),
        name="bias_tiles",
    )(rel_bias, bucket)


def _attn_kernel(bnd_ref, lq_ref, qt_ref, k_ref, vt_ref, bias_ref, gate_ref, gain_ref, o_ref,
                 q2_scr, m_scr, l_scr, acc_scr, kabs_scr, mreg_scr, pa_scr, pb_scr,
                 *, lambda_init, n_tiles):
    tq = qt_ref.shape[1] * qt_ref.shape[3]
    t = vt_ref.shape[-1]
    nqb = tq // t
    h = pl.program_id(0)
    i = pl.program_id(1)

    @pl.when(i == 0)
    def _():
        ka = jnp.max(jnp.abs(k_ref[...].astype(F32)), axis=0, keepdims=True)
        kabs_scr[...] = jnp.broadcast_to(ka, (8, HEAD_DIM)).T

    qt = jnp.concatenate([qt_ref[0, r] for r in range(qt_ref.shape[1])], axis=1)
    row = lax.broadcasted_iota(jnp.int32, qt.shape, 0)
    zero = jnp.zeros_like(qt)
    q2_scr[:, :tq] = jnp.where(row < HALF_DIM, qt, zero)
    q2_scr[:, tq:] = jnp.where(row >= HALF_DIM, qt, zero)
    l_scr[...] = jnp.zeros(l_scr.shape, F32)
    acc_scr[...] = jnp.zeros(acc_scr.shape, F32)

    c_left = bias_ref[0, 0, 0:1, t - 1:t]
    c_right = bias_ref[0, 2, t - 1:t, 0:1]

    wq = jnp.abs(qt.astype(F32)) * kabs_scr[:, 0:1]
    hb = jnp.concatenate([jnp.sum(wq[:HALF_DIM], axis=0, keepdims=True),
                          jnp.sum(wq[HALF_DIM:], axis=0, keepdims=True)], axis=1)
    b_max = bnd_ref[h, 0]
    b_min = bnd_ref[h, 1]
    fixed_ok = 2.0 * jnp.max(hb) + (b_max - b_min) <= EXP2_SPAN

    def block_relation(j, col0):
        e = j - (nqb * i + (col0 % tq) // t)
        tile_idx = jnp.where(jnp.abs(e) <= 1, e + 1, ZERO_BIAS_TILE)
        region = jnp.where(e < -1, 0, jnp.where(e > 1, 2, 1))
        return tile_idx, region

    def step(j_p, p_out, j_v, p_in, with_bias):
        if j_p is not None:
            kt = k_ref[pl.ds(pl.multiple_of(j_p * t, t), t), :]
        if j_v is not None:
            vt = vt_ref[0, j_v]
        for c0 in range(0, 2 * tq, MXU_COLS):
            cs = slice(c0, c0 + MXU_COLS)
            if j_p is not None:
                tile_idx, region = block_relation(j_p, c0)
                s = jnp.dot(kt, q2_scr[:, cs], preferred_element_type=F32)
                if with_bias:
                    b0 = c0 % t
                    s = s + bias_ref[0, tile_idx, :, b0:b0 + MXU_COLS]
                p = jnp.exp2(s - mreg_scr[region, :, cs])
                l_scr[:, cs] += jnp.sum(p, axis=0, keepdims=True)
                p_out[:, cs] = p.astype(BF16)
            if j_v is not None:
                acc_scr[:, cs] += jnp.dot(vt, p_in[:, cs], preferred_element_type=F32)

    def near_band(j):
        return jnp.logical_and(j >= nqb * i - 1, j <= nqb * i + nqb)

    def step_pair(j, last):
        def run(with_bias):
            step(j + 1, pb_scr, j, pa_scr, with_bias)
            step(None if last else j + 2, pa_scr, j + 1, pb_scr, with_bias)

        near = near_band(j + 1) if last else jnp.logical_or(near_band(j + 1), near_band(j + 2))
        lax.cond(near, lambda: run(True), lambda: run(False))

    @pl.when(fixed_ok)
    def _():
        m0 = hb + b_max
        mreg_scr[0] = m0 - c_left
        mreg_scr[1] = m0
        mreg_scr[2] = m0 - c_right

        def pair_body(jj, carry):
            step_pair(2 * jj, last=False)
            return carry

        step(0, pa_scr, None, None, True)
        lax.fori_loop(0, n_tiles // 2 - 1, pair_body, 0)
        step_pair(n_tiles - 2, last=True)

    def online_step(j):
        kt = k_ref[pl.ds(pl.multiple_of(j * t, t), t), :]
        vt = vt_ref[0, j]
        for c0 in range(0, 2 * tq, t):
            cs = slice(c0, c0 + t)
            tile_idx, region = block_relation(j, c0)
            c_far = jnp.where(region == 0, c_left, jnp.where(region == 2, c_right, 0.0))
            s = (jnp.dot(kt, q2_scr[:, cs], preferred_element_type=F32)
                 + (bias_ref[0, tile_idx] + c_far))
            m_old = m_scr[:, cs]
            m_new = jnp.maximum(m_old, jnp.max(s, axis=0, keepdims=True))
            p = jnp.exp2(s - m_new)
            alpha = jnp.exp2(m_old - m_new)
            l_scr[:, cs] = alpha * l_scr[:, cs] + jnp.sum(p, axis=0, keepdims=True)
            pv = jnp.dot(vt, p.astype(BF16), preferred_element_type=F32)
            acc_scr[:, cs] = alpha * acc_scr[:, cs] + pv
            m_scr[:, cs] = m_new

    @pl.when(jnp.logical_not(fixed_ok))
    def _():
        def body(j, carry):
            online_step(j)
            return carry

        m_scr[...] = jnp.full(m_scr.shape, NEG_INIT, F32)
        lax.fori_loop(0, n_tiles, body, 0)

    lq = lq_ref[...]
    lam = (jnp.exp(jnp.sum(lq[0:1] * lq[1:2], axis=-1, keepdims=True))
           - jnp.exp(jnp.sum(lq[2:3] * lq[3:4], axis=-1, keepdims=True)) + lambda_init)
    accn = acc_scr[...] * (1.0 / l_scr[...])
    o_t = accn[:, :tq] - lam * accn[:, tq:]
    o = o_t.T
    ms = jnp.mean(o * o, axis=-1, keepdims=True)
    y = ((o * lax.rsqrt(ms + EPS)) * gain_ref[...]) * (1.0 - lambda_init)
    o_ref[...] = (y * _silu(gate_ref[...].astype(F32))).astype(BF16)


def _attention(bnd, lq, qt4, proj, vt4, bias, gain, lambda_init):
    s = proj.shape[0]
    tq = TQ_ATT
    t = vt4.shape[-1]
    assert t == bias.shape[-1] and tq % t == 0
    n_tiles = s // t
    kblk = OFF_K // HEAD_DIM
    gblk = OFF_B_GATE // HEAD_DIM
    kern = functools.partial(_attn_kernel, lambda_init=lambda_init, n_tiles=n_tiles)
    return pl.pallas_call(
        kern,
        grid=(N_HEADS, s // tq),
        in_specs=[
            pl.BlockSpec(memory_space=pltpu.SMEM),
            pl.BlockSpec((4, HALF_DIM), lambda h, i: (0, 0)),
            pl.BlockSpec((1, tq // t, HEAD_DIM, t), lambda h, i: (h, i, 0, 0)),
            pl.BlockSpec((s, HEAD_DIM), lambda h, i: (0, kblk + h)),
            pl.BlockSpec((1, n_tiles, HEAD_DIM, t), lambda h, i: (h, 0, 0, 0)),
            pl.BlockSpec((1, N_BIAS_TILES, t, t), lambda h, i: (h, 0, 0, 0)),
            pl.BlockSpec((tq, HEAD_DIM), lambda h, i: (i, gblk + h)),
            pl.BlockSpec((1, HEAD_DIM), lambda h, i: (0, h)),
        ],
        out_specs=pl.BlockSpec((tq, HEAD_DIM), lambda h, i: (i, h)),
        out_shape=jax.ShapeDtypeStruct((s, D_DIFF), BF16),
        scratch_shapes=[
            pltpu.VMEM((HEAD_DIM, 2 * tq), BF16),
            pltpu.VMEM((1, 2 * tq), F32),
            pltpu.VMEM((1, 2 * tq), F32),
            pltpu.VMEM((HEAD_DIM, 2 * tq), F32),
            pltpu.VMEM((HEAD_DIM, 8), F32),
            pltpu.VMEM((3, 1, 2 * tq), F32),
            pltpu.VMEM((t, 2 * tq), BF16),
            pltpu.VMEM((t, 2 * tq), BF16),
        ],
        compiler_params=pltpu.CompilerParams(
            dimension_semantics=("arbitrary", "arbitrary"), vmem_limit_bytes=VMEM_LIMIT),
        name="diff_attention",
    )(bnd, lq, qt4, proj, vt4, bias, proj, gain)


def _outproj_kernel(ya_ref, ag_ref, yb_ref, u_ref, vsg_ref, cg_ref, x_ref, wout_ref, ws_ref,
                    bs_ref, vgain_ref, pgain_ref, o_ref, y_scr):
    tm = x_ref.shape[0]
    y_scr[:, 0:D_FOURIER] = (ya_ref[...] * _silu(ag_ref[...].astype(F32))).astype(BF16)
    y_scr[:, D_FOURIER:D_FOURIER + D_DIFF] = yb_ref[...]
    for g in range(D_GMLP // GROUP):
        cols = slice(g * GROUP, (g + 1) * GROUP)
        v = vsg_ref[:, cols].astype(F32)
        ms = jnp.mean(v * v, axis=-1, keepdims=True)
        vn = ((v * lax.rsqrt(ms + EPS)) * vgain_ref[:, cols]).astype(BF16)
        w = ws_ref[g]
        b = bs_ref[g]
        for c in range(tm // CHUNK):
            rows = slice(c * CHUNK, (c + 1) * CHUNK)
            mixed = jnp.dot(w, vn[rows], preferred_element_type=F32) + b
            gate = cg_ref[rows, cols].astype(F32)
            yc = (u_ref[rows, cols].astype(F32) * mixed) * _silu(gate)
            y_scr[rows, D_FOURIER + D_DIFF + g * GROUP:D_FOURIER + D_DIFF + (g + 1) * GROUP] = (
                yc.astype(BF16))
    y = jnp.dot(y_scr[...], wout_ref[...], preferred_element_type=F32)
    ms = jnp.mean(y * y, axis=-1, keepdims=True)
    o_ref[...] = x_ref[...] + (y * lax.rsqrt(ms + EPS)) * pgain_ref[...]


def _outproj(ya, yb, proj, x2, wout_bf, ws_bf, layer, bs3, vgain, pgain):
    s, d = x2.shape
    tm = TM_OUT
    ablk = OFF_A_GATE // D_FOURIER
    ublk, vblk, cblk = OFF_U // D_GMLP, OFF_VSG // D_GMLP, OFF_C_GATE // D_GMLP
    const = lambda shape: pl.BlockSpec(shape, lambda i: tuple(0 for _ in shape))
    return pl.pallas_call(
        _outproj_kernel,
        grid=(s // tm,),
        in_specs=[
            pl.BlockSpec((tm, D_FOURIER), lambda i: (i, 0)),
            pl.BlockSpec((tm, D_FOURIER), lambda i: (i, ablk)),
            pl.BlockSpec((tm, D_DIFF), lambda i: (i, 0)),
            pl.BlockSpec((tm, D_GMLP), lambda i: (i, ublk)),
            pl.BlockSpec((tm, D_GMLP), lambda i: (i, vblk)),
            pl.BlockSpec((tm, D_GMLP), lambda i: (i, cblk)),
            pl.BlockSpec((tm, d), lambda i: (i, 0)),
            pl.BlockSpec((None, d, d), lambda i: (layer, 0, 0)),
            pl.BlockSpec((None,) + ws_bf.shape[1:], lambda i: (layer, 0, 0, 0)),
            const(bs3.shape),
            const((1, D_GMLP)),
            const((1, d)),
        ],
        out_specs=pl.BlockSpec((tm, d), lambda i: (i, 0)),
        out_shape=jax.ShapeDtypeStruct((s, d), F32),
        scratch_shapes=[pltpu.VMEM((tm, d), BF16)],
        compiler_params=pltpu.CompilerParams(
            dimension_semantics=("arbitrary",), vmem_limit_bytes=VMEM_LIMIT),
        name="outproj",
    )(ya, proj, yb, proj, proj, proj, x2, wout_bf, ws_bf, bs3, vgain, pgain)


def kernel(x, w_in, pre_gain, post_gain, w_fourier, lambda_qk, diff_out_gain, sg_v_gain,
           w_spatial, b_spatial, w_out, rel_bias):
    b, s, d = x.shape
    depth = w_in.shape[0]
    assert b == 1 and w_in.shape[2] == D_IN
    x2 = x.reshape(s, d)

    q_scale = (HALF_DIM ** -0.5) * LOG2E
    colscale = jnp.ones((1, D_IN), F32).at[:, OFF_Q:OFF_Q + D_DIFF].set(q_scale)
    bias = _bias_tiles(rel_bias, T_ATT)
    bias_bounds = jnp.stack([jnp.max(rel_bias, axis=0), jnp.min(rel_bias, axis=0)], axis=1) * LOG2E
    consts = _dft_consts(s // FFT_N2, FFT_N2, GROUP)

    w_in_bf, w_out_bf = w_in.astype(BF16), w_out.astype(BF16)
    w_fourier_bf, w_spatial_bf = w_fourier.astype(BF16), w_spatial.astype(BF16)

    for l in range(depth):
        lambda_init = 0.8 - 0.6 * math.exp(-0.3 * l)
        proj, a_in, qt4, vt4 = _inproj(x2, pre_gain[l].reshape(1, d), w_in_bf, l, colscale)
        ya = _fourier_branch(a_in, w_fourier_bf, l, consts)
        yb = _attention(bias_bounds, lambda_qk[l], qt4, proj, vt4, bias,
                        diff_out_gain[l].reshape(1, D_DIFF), lambda_init)
        x2 = _outproj(ya, yb, proj, x2, w_out_bf, w_spatial_bf, l,
                      b_spatial[l].reshape(D_GMLP // GROUP, CHUNK, 1),
                      sg_v_gain[l].reshape(1, D_GMLP), post_gain[l].reshape(1, d))
    return x2.reshape(b, s, d)
```

```python
import functools
import math

import numpy as np
import jax
import jax.numpy as jnp
from jax import lax
from jax.experimental import pallas as pl
from jax.experimental.pallas import tpu as pltpu

F32 = jnp.float32
BF16 = jnp.bfloat16

EPS = 1e-6
LOG2E = math.log2(math.e)
NEG_INIT = -1e30
EXP2_SPAN = 120.0

D_FOURIER = 512
D_DIFF = 1024
D_GMLP = 512
GROUP = 128
HEAD_DIM = 128
HALF_DIM = 64
N_HEADS = 8
CHUNK = 128
N_BUCKETS = 32
MAX_DISTANCE = 128
OFF_A_IN, OFF_A_GATE, OFF_Q, OFF_K, OFF_V, OFF_B_GATE, OFF_U, OFF_VSG, OFF_C_GATE = (
    0, 512, 1024, 2048, 3072, 4096, 5120, 5632, 6144)
D_IN = 6656

MXU_COLS = 256
TM_IN = 512
TN_IN = 3328
TM_OUT = 512
T_ATT = 512
TQ_ATT = 2048
N_BIAS_TILES = 4
ZERO_BIAS_TILE = 3
FFT_N2 = 128
FFT_ROWS = 8
FFT_STEP_ROWS = 32
VMEM_LIMIT = 56 * 1024 * 1024
VMEM_LIMIT_INPROJ = 62 * 1024 * 1024
assert TM_IN % T_ATT == 0


def _silu(x):
    return x * (1.0 / (1.0 + jnp.exp(-x)))


def _inproj_kernel(x_ref, g_ref, w_ref, cs_ref, proj_ref, ain_ref, qt_ref, vt_ref, a_scr):
    j = pl.program_id(1)

    @pl.when(j == 0)
    def _():
        x = x_ref[...]
        ms = jnp.mean(x * x, axis=-1, keepdims=True)
        a_scr[...] = ((x * lax.rsqrt(ms + EPS)) * g_ref[...]).astype(BF16)

    acc = jnp.dot(a_scr[...], w_ref[...], preferred_element_type=F32)
    scaled = acc * cs_ref[...]
    proj_ref[...] = scaled.astype(BF16)

    @pl.when(j == 0)
    def _():
        ain_ref[...] = acc[:, :D_FOURIER]

    tn = proj_ref.shape[1]
    for jj in range(D_IN // tn):
        heads = [(dst, (c - off) // HEAD_DIM, c - jj * tn)
                 for dst, off in ((qt_ref, OFF_Q), (vt_ref, OFF_V))
                 for c in range(off, off + D_DIFF, HEAD_DIM)
                 if jj * tn <= c < (jj + 1) * tn]
        if heads:
            @pl.when(j == jj)
            def _(heads=heads):
                for dst, h, c0 in heads:
                    for r in range(dst.shape[1]):
                        rows = slice(r * T_ATT, (r + 1) * T_ATT)
                        dst[h, r] = scaled[rows, c0:c0 + HEAD_DIM].T.astype(BF16)


def _inproj(x2, gain, w_bf, layer, colscale):
    s, d = x2.shape
    n = w_bf.shape[2]
    grid = (s // TM_IN, n // TN_IN)
    head_tiles = pl.BlockSpec((N_HEADS, TM_IN // T_ATT, HEAD_DIM, T_ATT),
                              lambda i, j: (0, i, 0, 0))
    return pl.pallas_call(
        _inproj_kernel,
        grid=grid,
        in_specs=[
            pl.BlockSpec((TM_IN, d), lambda i, j: (i, 0)),
            pl.BlockSpec((1, d), lambda i, j: (0, 0)),
            pl.BlockSpec((None, d, TN_IN), lambda i, j: (layer, 0, j)),
            pl.BlockSpec((1, TN_IN), lambda i, j: (0, j)),
        ],
        out_specs=[
            pl.BlockSpec((TM_IN, TN_IN), lambda i, j: (i, j)),
            pl.BlockSpec((TM_IN, D_FOURIER), lambda i, j: (i, 0)),
            head_tiles,
            head_tiles,
        ],
        out_shape=[
            jax.ShapeDtypeStruct((s, n), BF16),
            jax.ShapeDtypeStruct((s, D_FOURIER), F32),
            jax.ShapeDtypeStruct((N_HEADS, s // T_ATT, HEAD_DIM, T_ATT), BF16),
            jax.ShapeDtypeStruct((N_HEADS, s // T_ATT, HEAD_DIM, T_ATT), BF16),
        ],
        scratch_shapes=[pltpu.VMEM((TM_IN, d), BF16)],
        compiler_params=pltpu.CompilerParams(
            dimension_semantics=("arbitrary", "arbitrary"),
            vmem_limit_bytes=VMEM_LIMIT_INPROJ),
        name="inproj",
    )(x2, gain, w_bf, colscale)


def _dft_consts(n1, n2, c):
    def cs(n):
        idx = np.arange(n)
        ang = 2.0 * np.pi * ((idx[:, None] * idx[None, :]) % n) / n
        return np.cos(ang), np.sin(ang)

    c1, s1 = cs(n1)
    c2, s2 = cs(n2)
    cc, sc = cs(c)
    f1 = np.kron(np.concatenate([c1, -s1], axis=0), np.eye(FFT_ROWS))
    f2 = np.block([[c2, s2], [s2, -c2]])
    n = n1 * n2
    k1 = np.arange(n1)[:, None]
    m2 = np.arange(n2)[None, :]
    ang = 2.0 * np.pi * ((k1 * m2) % n) / n
    tw_c = np.cos(ang)[:, :, None]
    tw_s = np.sin(ang)[:, :, None]
    ch = np.concatenate([cc, -sc], axis=0)
    f32 = lambda a: jnp.asarray(a.astype(np.float32))
    return f32(f1), f32(f2), f32(tw_c), f32(tw_s), f32(ch)


def _fft1_kernel(f1_ref, x_ref, y_ref):
    n1, rows, c = x_ref.shape
    f1 = f1_ref[...].astype(BF16)
    for r0 in range(0, rows, FFT_ROWS):
        blk = slice(r0, r0 + FFT_ROWS)
        x = x_ref[:, blk, :].reshape(n1 * FFT_ROWS, c).astype(BF16)
        y = jnp.dot(f1, x, preferred_element_type=F32)
        y_ref[:, blk, :] = y.reshape(y_ref.shape[0], FFT_ROWS, c)


def _fft1(f1, x3):
    n1, n2, c = x3.shape
    return pl.pallas_call(
        _fft1_kernel,
        grid=(n2 // FFT_STEP_ROWS,),
        in_specs=[pl.BlockSpec(f1.shape, lambda j: (0, 0)),
                  pl.BlockSpec((n1, FFT_STEP_ROWS, c), lambda j: (0, j, 0))],
        out_specs=pl.BlockSpec((2 * n1, FFT_STEP_ROWS, c), lambda j: (0, j, 0)),
        out_shape=jax.ShapeDtypeStruct((2 * n1, n2, c), F32),
        compiler_params=pltpu.CompilerParams(
            dimension_semantics=("arbitrary",), vmem_limit_bytes=VMEM_LIMIT),
        name="fft_stage1",
    )(f1, x3)


def _fft2_kernel(y_ref, twc_ref, tws_ref, f2_ref, ch_ref, wf_ref, o_ref, *, scale):
    f2 = f2_ref[...].astype(BF16)
    ch = ch_ref[...].astype(BF16)
    _, rows, n2, c = y_ref.shape
    ps, qs = [], []
    for r in range(rows):
        yr = y_ref[0, r]
        yi = y_ref[1, r]
        tc = twc_ref[r]
        ts = tws_ref[r]
        zr = yr * tc + yi * ts
        zi = yi * tc - yr * ts
        z = jnp.concatenate([zr, zi], axis=0).astype(BF16)
        pq = jnp.dot(f2, z, preferred_element_type=F32)
        ps.append(pq[:n2].astype(BF16))
        qs.append(pq[n2:].astype(BF16))
    p = jnp.concatenate(ps, axis=0)
    q = jnp.concatenate(qs, axis=0)
    fs = []
    for g in range(c // GROUP):
        cols = slice(g * GROUP, (g + 1) * GROUP)
        pq_g = jnp.concatenate([p[:, cols], q[:, cols]], axis=1)
        fs.append(jnp.dot(pq_g, ch, preferred_element_type=F32))
    f = (jnp.concatenate(fs, axis=1) * scale).astype(BF16)
    ya = jnp.dot(f, wf_ref[...], preferred_element_type=F32)
    for r in range(rows):
        o_ref[:, r, :] = ya[r * n2:(r + 1) * n2]


def _fft2(y4, twc, tws, f2, ch, wf_bf, layer, scale):
    _, n1, n2, c = y4.shape
    const = lambda shape: pl.BlockSpec(shape, lambda k: tuple(0 for _ in shape))
    return pl.pallas_call(
        functools.partial(_fft2_kernel, scale=scale),
        grid=(n1 // FFT_ROWS,),
        in_specs=[
            pl.BlockSpec((2, FFT_ROWS, n2, c), lambda k: (0, k, 0, 0)),
            pl.BlockSpec((FFT_ROWS, n2, 1), lambda k: (k, 0, 0)),
            pl.BlockSpec((FFT_ROWS, n2, 1), lambda k: (k, 0, 0)),
            const((2 * n2, 2 * n2)),
            const(ch.shape),
            pl.BlockSpec((None, c, c), lambda k: (layer, 0, 0)),
        ],
        out_specs=pl.BlockSpec((n2, FFT_ROWS, c), lambda k: (0, k, 0)),
        out_shape=jax.ShapeDtypeStruct((n2, n1, c), F32),
        compiler_params=pltpu.CompilerParams(
            dimension_semantics=("arbitrary",), vmem_limit_bytes=VMEM_LIMIT),
        name="fft_stage2",
    )(y4, twc, tws, f2, ch, wf_bf)


def _fourier_branch(a_in, wf_bf, layer, consts):
    s, c = a_in.shape
    n2 = FFT_N2
    n1 = s // n2
    f1, f2, twc, tws, ch = consts
    y = _fft1(f1, a_in.reshape(n1, n2, c))
    scale = 1.0 / math.sqrt(s * GROUP)
    out = _fft2(y.reshape(2, n1, n2, c), twc, tws, f2, ch, wf_bf, layer, scale)
    return out.reshape(s, c)


def _t5_bucket(rel):
    nb = N_BUCKETS // 2
    ret = (rel > 0).astype(jnp.int32) * nb
    n = jnp.abs(rel)
    max_exact = nb // 2
    nf = jnp.maximum(n, 1).astype(jnp.float32)
    large = max_exact + (jnp.log(nf / max_exact) / math.log(MAX_DISTANCE / max_exact)
                         * (nb - max_exact)).astype(jnp.int32)
    large = jnp.minimum(large, nb - 1)
    return ret + jnp.where(n < max_exact, n, large)


def _bias_kernel(rb_ref, bk_ref, o_ref):
    h = pl.program_id(0)
    half = N_BUCKETS // 2
    t = bk_ref.shape[1]
    blk = MAX_DISTANCE
    for tile, d in enumerate((-1, 0, 1)):
        for kb in range(t // blk):
            for qb in range(t // blk):
                rel_lo = d * t + kb * blk - (qb * blk + blk - 1)
                rel_hi = d * t + kb * blk + blk - 1 - qb * blk
                rows, cols = slice(kb * blk, (kb + 1) * blk), slice(qb * blk, (qb + 1) * blk)
                if rel_hi <= -blk:
                    val = jnp.full((blk, blk), rb_ref[half - 1, h], F32)
                elif rel_lo >= blk:
                    val = jnp.full((blk, blk), rb_ref[N_BUCKETS - 1, h], F32)
                else:
                    bk = bk_ref[tile, rows, cols]
                    val = jnp.zeros(bk.shape, F32)
                    for b in range(0 if rel_lo <= 0 else half, half if rel_hi < 0 else N_BUCKETS):
                        val = jnp.where(bk == b, rb_ref[b, h], val)
                o_ref[0, tile, rows, cols] = val * LOG2E
    o_ref[0, ZERO_BIAS_TILE] = jnp.zeros((t, t), F32)


def _bias_tiles(rel_bias, t):
    kk = jnp.arange(t, dtype=jnp.int32)[:, None]
    rr = jnp.arange(t, dtype=jnp.int32)[None, :]
    rel = jnp.stack([d * t + kk - rr for d in (-1, 0, 1)])
    bucket = _t5_bucket(rel)
    return pl.pallas_call(
        _bias_kernel,
        grid=(N_HEADS,),
        in_specs=[pl.BlockSpec(memory_space=pltpu.SMEM),
                  pl.BlockSpec(bucket.shape, lambda h: (0, 0, 0))],
        out_specs=pl.BlockSpec((1, N_BIAS_TILES, t, t), lambda h: (h, 0, 0, 0)),
        out_shape=jax.ShapeDtypeStruct((N_HEADS, N_BIAS_TILES, t, t), F32),
        compiler_params=pltpu.CompilerParams(
            dimension_semantics=("arbitrary",), vmem_limit_bytes=VMEM_LIMIT),
        name="bias_tiles",
    )(rel_bias, bucket)


def _attn_kernel(bnd_ref, lq_ref, qt_ref, k_ref, vt_ref, bias_ref, gate_ref, gain_ref, o_ref,
                 q2_scr, m_scr, l_scr, acc_scr, kabs_scr, mreg_scr, pa_scr, pb_scr,
                 *, lambda_init, n_tiles):
    tq = qt_ref.shape[1] * qt_ref.shape[3]
    t = vt_ref.shape[-1]
    nqb = tq // t
    h = pl.program_id(0)
    i = pl.program_id(1)

    @pl.when(i == 0)
    def _():
        ka = jnp.max(jnp.abs(k_ref[...].astype(F32)), axis=0, keepdims=True)
        kabs_scr[...] = jnp.broadcast_to(ka, (8, HEAD_DIM)).T

    qt = jnp.concatenate([qt_ref[0, r] for r in range(qt_ref.shape[1])], axis=1)
    row = lax.broadcasted_iota(jnp.int32, qt.shape, 0)
    zero = jnp.zeros_like(qt)
    q2_scr[:, :tq] = jnp.where(row < HALF_DIM, qt, zero)
    q2_scr[:, tq:] = jnp.where(row >= HALF_DIM, qt, zero)
    l_scr[...] = jnp.zeros(l_scr.shape, F32)
    acc_scr[...] = jnp.zeros(acc_scr.shape, F32)

    c_left = bias_ref[0, 0, 0:1, t - 1:t]
    c_right = bias_ref[0, 2, t - 1:t, 0:1]

    wq = jnp.abs(qt.astype(F32)) * kabs_scr[:, 0:1]
    hb = jnp.concatenate([jnp.sum(wq[:HALF_DIM], axis=0, keepdims=True),
                          jnp.sum(wq[HALF_DIM:], axis=0, keepdims=True)], axis=1)
    b_max = bnd_ref[h, 0]
    b_min = bnd_ref[h, 1]
    fixed_ok = 2.0 * jnp.max(hb) + (b_max - b_min) <= EXP2_SPAN

    def block_relation(j, col0):
        e = j - (nqb * i + (col0 % tq) // t)
        tile_idx = jnp.where(jnp.abs(e) <= 1, e + 1, ZERO_BIAS_TILE)
        region = jnp.where(e < -1, 0, jnp.where(e > 1, 2, 1))
        return tile_idx, region

    def step(j_p, p_out, j_v, p_in, with_bias):
        if j_p is not None:
            kt = k_ref[pl.ds(pl.multiple_of(j_p * t, t), t), :]
        if j_v is not None:
            vt = vt_ref[0, j_v]
        for c0 in range(0, 2 * tq, MXU_COLS):
            cs = slice(c0, c0 + MXU_COLS)
            if j_p is not None:
                tile_idx, region = block_relation(j_p, c0)
                s = jnp.dot(kt, q2_scr[:, cs], preferred_element_type=F32)
                if with_bias:
                    b0 = c0 % t
                    s = s + bias_ref[0, tile_idx, :, b0:b0 + MXU_COLS]
                p = jnp.exp2(s - mreg_scr[region, :, cs])
                l_scr[:, cs] += jnp.sum(p, axis=0, keepdims=True)
                p_out[:, cs] = p.astype(BF16)
            if j_v is not None:
                acc_scr[:, cs] += jnp.dot(vt, p_in[:, cs], preferred_element_type=F32)

    def near_band(j):
        return jnp.logical_and(j >= nqb * i - 1, j <= nqb * i + nqb)

    def step_pair(j, last):
        def run(with_bias):
            step(j + 1, pb_scr, j, pa_scr, with_bias)
            step(None if last else j + 2, pa_scr, j + 1, pb_scr, with_bias)

        near = near_band(j + 1) if last else jnp.logical_or(near_band(j + 1), near_band(j + 2))
        lax.cond(near, lambda: run(True), lambda: run(False))

    @pl.when(fixed_ok)
    def _():
        m0 = hb + b_max
        mreg_scr[0] = m0 - c_left
        mreg_scr[1] = m0
        mreg_scr[2] = m0 - c_right

        def pair_body(jj, carry):
            step_pair(2 * jj, last=False)
            return carry

        step(0, pa_scr, None, None, True)
        lax.fori_loop(0, n_tiles // 2 - 1, pair_body, 0)
        step_pair(n_tiles - 2, last=True)

    def online_step(j):
        kt = k_ref[pl.ds(pl.multiple_of(j * t, t), t), :]
        vt = vt_ref[0, j]
        for c0 in range(0, 2 * tq, t):
            cs = slice(c0, c0 + t)
            tile_idx, region = block_relation(j, c0)
            c_far = jnp.where(region == 0, c_left, jnp.where(region == 2, c_right, 0.0))
            s = (jnp.dot(kt, q2_scr[:, cs], preferred_element_type=F32)
                 + (bias_ref[0, tile_idx] + c_far))
            m_old = m_scr[:, cs]
            m_new = jnp.maximum(m_old, jnp.max(s, axis=0, keepdims=True))
            p = jnp.exp2(s - m_new)
            alpha = jnp.exp2(m_old - m_new)
            l_scr[:, cs] = alpha * l_scr[:, cs] + jnp.sum(p, axis=0, keepdims=True)
            pv = jnp.dot(vt, p.astype(BF16), preferred_element_type=F32)
            acc_scr[:, cs] = alpha * acc_scr[:, cs] + pv
            m_scr[:, cs] = m_new

    @pl.when(jnp.logical_not(fixed_ok))
    def _():
        def body(j, carry):
            online_step(j)
            return carry

        m_scr[...] = jnp.full(m_scr.shape, NEG_INIT, F32)
        lax.fori_loop(0, n_tiles, body, 0)

    lq = lq_ref[...]
    lam = (jnp.exp(jnp.sum(lq[0:1] * lq[1:2], axis=-1, keepdims=True))
           - jnp.exp(jnp.sum(lq[2:3] * lq[3:4], axis=-1, keepdims=True)) + lambda_init)
    accn = acc_scr[...] * (1.0 / l_scr[...])
    o_t = accn[:, :tq] - lam * accn[:, tq:]
    o = o_t.T
    ms = jnp.mean(o * o, axis=-1, keepdims=True)
    y = ((o * lax.rsqrt(ms + EPS)) * gain_ref[...]) * (1.0 - lambda_init)
    o_ref[...] = (y * _silu(gate_ref[...].astype(F32))).astype(BF16)


def _attention(bnd, lq, qt4, proj, vt4, bias, gain, lambda_init):
    s = proj.shape[0]
    tq = TQ_ATT
    t = vt4.shape[-1]
    assert t == bias.shape[-1] and tq % t == 0
    n_tiles = s // t
    kblk = OFF_K // HEAD_DIM
    gblk = OFF_B_GATE // HEAD_DIM
    kern = functools.partial(_attn_kernel, lambda_init=lambda_init, n_tiles=n_tiles)
    return pl.pallas_call(
        kern,
        grid=(N_HEADS, s // tq),
        in_specs=[
            pl.BlockSpec(memory_space=pltpu.SMEM),
            pl.BlockSpec((4, HALF_DIM), lambda h, i: (0, 0)),
            pl.BlockSpec((1, tq // t, HEAD_DIM, t), lambda h, i: (h, i, 0, 0)),
            pl.BlockSpec((s, HEAD_DIM), lambda h, i: (0, kblk + h)),
            pl.BlockSpec((1, n_tiles, HEAD_DIM, t), lambda h, i: (h, 0, 0, 0)),
            pl.BlockSpec((1, N_BIAS_TILES, t, t), lambda h, i: (h, 0, 0, 0)),
            pl.BlockSpec((tq, HEAD_DIM), lambda h, i: (i, gblk + h)),
            pl.BlockSpec((1, HEAD_DIM), lambda h, i: (0, h)),
        ],
        out_specs=pl.BlockSpec((tq, HEAD_DIM), lambda h, i: (i, h)),
        out_shape=jax.ShapeDtypeStruct((s, D_DIFF), BF16),
        scratch_shapes=[
            pltpu.VMEM((HEAD_DIM, 2 * tq), BF16),
            pltpu.VMEM((1, 2 * tq), F32),
            pltpu.VMEM((1, 2 * tq), F32),
            pltpu.VMEM((HEAD_DIM, 2 * tq), F32),
            pltpu.VMEM((HEAD_DIM, 8), F32),
            pltpu.VMEM((3, 1, 2 * tq), F32),
            pltpu.VMEM((t, 2 * tq), BF16),
            pltpu.VMEM((t, 2 * tq), BF16),
        ],
        compiler_params=pltpu.CompilerParams(
            dimension_semantics=("arbitrary", "arbitrary"), vmem_limit_bytes=VMEM_LIMIT),
        name="diff_attention",
    )(bnd, lq, qt4, proj, vt4, bias, proj, gain)


def _outproj_kernel(ya_ref, ag_ref, yb_ref, u_ref, vsg_ref, cg_ref, x_ref, wout_ref, ws_ref,
                    bs_ref, vgain_ref, pgain_ref, o_ref, y_scr):
    tm = x_ref.shape[0]
    y_scr[:, 0:D_FOURIER] = (ya_ref[...] * _silu(ag_ref[...].astype(F32))).astype(BF16)
    y_scr[:, D_FOURIER:D_FOURIER + D_DIFF] = yb_ref[...]
    for g in range(D_GMLP // GROUP):
        cols = slice(g * GROUP, (g + 1) * GROUP)
        v = vsg_ref[:, cols].astype(F32)
        ms = jnp.mean(v * v, axis=-1, keepdims=True)
        vn = ((v * lax.rsqrt(ms + EPS)) * vgain_ref[:, cols]).astype(BF16)
        w = ws_ref[g]
        b = bs_ref[g]
        for c in range(tm // CHUNK):
            rows = slice(c * CHUNK, (c + 1) * CHUNK)
            mixed = jnp.dot(w, vn[rows], preferred_element_type=F32) + b
            gate = cg_ref[rows, cols].astype(F32)
            yc = (u_ref[rows, cols].astype(F32) * mixed) * _silu(gate)
            y_scr[rows, D_FOURIER + D_DIFF + g * GROUP:D_FOURIER + D_DIFF + (g + 1) * GROUP] = (
                yc.astype(BF16))
    y = jnp.dot(y_scr[...], wout_ref[...], preferred_element_type=F32)
    ms = jnp.mean(y * y, axis=-1, keepdims=True)
    o_ref[...] = x_ref[...] + (y * lax.rsqrt(ms + EPS)) * pgain_ref[...]


def _outproj(ya, yb, proj, x2, wout_bf, ws_bf, layer, bs3, vgain, pgain):
    s, d = x2.shape
    tm = TM_OUT
    ablk = OFF_A_GATE // D_FOURIER
    ublk, vblk, cblk = OFF_U // D_GMLP, OFF_VSG // D_GMLP, OFF_C_GATE // D_GMLP
    const = lambda shape: pl.BlockSpec(shape, lambda i: tuple(0 for _ in shape))
    return pl.pallas_call(
        _outproj_kernel,
        grid=(s // tm,),
        in_specs=[
            pl.BlockSpec((tm, D_FOURIER), lambda i: (i, 0)),
            pl.BlockSpec((tm, D_FOURIER), lambda i: (i, ablk)),
            pl.BlockSpec((tm, D_DIFF), lambda i: (i, 0)),
            pl.BlockSpec((tm, D_GMLP), lambda i: (i, ublk)),
            pl.BlockSpec((tm, D_GMLP), lambda i: (i, vblk)),
            pl.BlockSpec((tm, D_GMLP), lambda i: (i, cblk)),
            pl.BlockSpec((tm, d), lambda i: (i, 0)),
            pl.BlockSpec((None, d, d), lambda i: (layer, 0, 0)),
            pl.BlockSpec((None,) + ws_bf.shape[1:], lambda i: (layer, 0, 0, 0)),
            const(bs3.shape),
            const((1, D_GMLP)),
            const((1, d)),
        ],
        out_specs=pl.BlockSpec((tm, d), lambda i: (i, 0)),
        out_shape=jax.ShapeDtypeStruct((s, d), F32),
        scratch_shapes=[pltpu.VMEM((tm, d), BF16)],
        compiler_params=pltpu.CompilerParams(
            dimension_semantics=("arbitrary",), vmem_limit_bytes=VMEM_LIMIT),
        name="outproj",
    )(ya, proj, yb, proj, proj, proj, x2, wout_bf, ws_bf, bs3, vgain, pgain)


def kernel(x, w_in, pre_gain, post_gain, w_fourier, lambda_qk, diff_out_gain, sg_v_gain,
           w_spatial, b_spatial, w_out, rel_bias):
    b, s, d = x.shape
    depth = w_in.shape[0]
    assert b == 1 and w_in.shape[2] == D_IN
    x2 = x.reshape(s, d)

    q_scale = (HALF_DIM ** -0.5) * LOG2E
    colscale = jnp.ones((1, D_IN), F32).at[:, OFF_Q:OFF_Q + D_DIFF].set(q_scale)
    bias = _bias_tiles(rel_bias, T_ATT)
    bias_bounds = jnp.stack([jnp.max(rel_bias, axis=0), jnp.min(rel_bias, axis=0)], axis=1) * LOG2E
    consts = _dft_consts(s // FFT_N2, FFT_N2, GROUP)

    w_in_bf, w_out_bf = w_in.astype(BF16), w_out.astype(BF16)
    w_fourier_bf, w_spatial_bf = w_fourier.astype(BF16), w_spatial.astype(BF16)

    for l in range(depth):
        lambda_init = 0.8 - 0.6 * math.exp(-0.3 * l)
        proj, a_in, qt4, vt4 = _inproj(x2, pre_gain[l].reshape(1, d), w_in_bf, l, colscale)
        ya = _fourier_branch(a_in, w_fourier_bf, l, consts)
        yb = _attention(bias_bounds, lambda_qk[l], qt4, proj, vt4, bias,
                        diff_out_gain[l].reshape(1, D_DIFF), lambda_init)
        x2 = _outproj(ya, yb, proj, x2, w_out_bf, w_spatial_bf, l,
                      b_spatial[l].reshape(D_GMLP // GROUP, CHUNK, 1),
                      sg_v_gain[l].reshape(1, D_GMLP), post_gain[l].reshape(1, d))
    return x2.reshape(b, s, d)
```

```python
import functools
import math

import numpy as np
import jax
import jax.numpy as jnp
from jax import lax
from jax.experimental import pallas as pl
from jax.experimental.pallas import tpu as pltpu

F32 = jnp.float32
BF16 = jnp.bfloat16

EPS = 1e-6
LOG2E = math.log2(math.e)
NEG_INIT = -1e30
EXP2_SPAN = 120.0

D_FOURIER = 512
D_DIFF = 1024
D_GMLP = 512
GROUP = 128
HEAD_DIM = 128
HALF_DIM = 64
N_HEADS = 8
CHUNK = 128
N_BUCKETS = 32
MAX_DISTANCE = 128
OFF_A_IN, OFF_A_GATE, OFF_Q, OFF_K, OFF_V, OFF_B_GATE, OFF_U, OFF_VSG, OFF_C_GATE = (
    0, 512, 1024, 2048, 3072, 4096, 5120, 5632, 6144)
D_IN = 6656

MXU_COLS = 256
TM_IN = 512
TN_IN = 3328
TM_OUT = 512
T_ATT = 512
TQ_ATT = 2048
N_BIAS_TILES = 4
ZERO_BIAS_TILE = 3
FFT_N2 = 128
FFT_ROWS = 8
FFT_STEP_ROWS = 32
VMEM_LIMIT = 56 * 1024 * 1024
VMEM_LIMIT_INPROJ = 62 * 1024 * 1024
assert TM_IN % T_ATT == 0


def _silu(x):
    return x * (1.0 / (1.0 + jnp.exp(-x)))


def _inproj_kernel(x_ref, g_ref, w_ref, cs_ref, proj_ref, ain_ref, qt_ref, vt_ref, a_scr):
    j = pl.program_id(1)

    @pl.when(j == 0)
    def _():
        x = x_ref[...]
        ms = jnp.mean(x * x, axis=-1, keepdims=True)
        a_scr[...] = ((x * lax.rsqrt(ms + EPS)) * g_ref[...]).astype(BF16)

    acc = jnp.dot(a_scr[...], w_ref[...], preferred_element_type=F32)
    scaled = acc * cs_ref[...]
    proj_ref[...] = scaled.astype(BF16)

    @pl.when(j == 0)
    def _():
        ain_ref[...] = acc[:, :D_FOURIER]

    tn = proj_ref.shape[1]
    for jj in range(D_IN // tn):
        heads = [(dst, (c - off) // HEAD_DIM, c - jj * tn)
                 for dst, off in ((qt_ref, OFF_Q), (vt_ref, OFF_V))
                 for c in range(off, off + D_DIFF, HEAD_DIM)
                 if jj * tn <= c < (jj + 1) * tn]
        if heads:
            @pl.when(j == jj)
            def _(heads=heads):
                for dst, h, c0 in heads:
                    for r in range(dst.shape[1]):
                        rows = slice(r * T_ATT, (r + 1) * T_ATT)
                        dst[h, r] = scaled[rows, c0:c0 + HEAD_DIM].T.astype(BF16)


def _inproj(x2, gain, w_bf, layer, colscale):
    s, d = x2.shape
    n = w_bf.shape[2]
    grid = (s // TM_IN, n // TN_IN)
    head_tiles = pl.BlockSpec((N_HEADS, TM_IN // T_ATT, HEAD_DIM, T_ATT),
                              lambda i, j: (0, i, 0, 0))
    return pl.pallas_call(
        _inproj_kernel,
        grid=grid,
        in_specs=[
            pl.BlockSpec((TM_IN, d), lambda i, j: (i, 0)),
            pl.BlockSpec((1, d), lambda i, j: (0, 0)),
            pl.BlockSpec((None, d, TN_IN), lambda i, j: (layer, 0, j)),
            pl.BlockSpec((1, TN_IN), lambda i, j: (0, j)),
        ],
        out_specs=[
            pl.BlockSpec((TM_IN, TN_IN), lambda i, j: (i, j)),
            pl.BlockSpec((TM_IN, D_FOURIER), lambda i, j: (i, 0)),
            head_tiles,
            head_tiles,
        ],
        out_shape=[
            jax.ShapeDtypeStruct((s, n), BF16),
            jax.ShapeDtypeStruct((s, D_FOURIER), F32),
            jax.ShapeDtypeStruct((N_HEADS, s // T_ATT, HEAD_DIM, T_ATT), BF16),
            jax.ShapeDtypeStruct((N_HEADS, s // T_ATT, HEAD_DIM, T_ATT), BF16),
        ],
        scratch_shapes=[pltpu.VMEM((TM_IN, d), BF16)],
        compiler_params=pltpu.CompilerParams(
            dimension_semantics=("arbitrary", "arbitrary"),
            vmem_limit_bytes=VMEM_LIMIT_INPROJ),
        name="inproj",
    )(x2, gain, w_bf, colscale)


def _dft_consts(n1, n2, c):
    def cs(n):
        idx = np.arange(n)
        ang = 2.0 * np.pi * ((idx[:, None] * idx[None, :]) % n) / n
        return np.cos(ang), np.sin(ang)

    c1, s1 = cs(n1)
    c2, s2 = cs(n2)
    cc, sc = cs(c)
    f1 = np.kron(np.concatenate([c1, -s1], axis=0), np.eye(FFT_ROWS))
    f2 = np.block([[c2, s2], [s2, -c2]])
    n = n1 * n2
    k1 = np.arange(n1)[:, None]
    m2 = np.arange(n2)[None, :]
    ang = 2.0 * np.pi * ((k1 * m2) % n) / n
    tw_c = np.cos(ang)[:, :, None]
    tw_s = np.sin(ang)[:, :, None]
    ch = np.concatenate([cc, -sc], axis=0)
    f32 = lambda a: jnp.asarray(a.astype(np.float32))
    return f32(f1), f32(f2), f32(tw_c), f32(tw_s), f32(ch)


def _fft1_kernel(f1_ref, x_ref, y_ref):
    n1, rows, c = x_ref.shape
    f1 = f1_ref[...].astype(BF16)
    for r0 in range(0, rows, FFT_ROWS):
        blk = slice(r0, r0 + FFT_ROWS)
        x = x_ref[:, blk, :].reshape(n1 * FFT_ROWS, c).astype(BF16)
        y = jnp.dot(f1, x, preferred_element_type=F32)
        y_ref[:, blk, :] = y.reshape(y_ref.shape[0], FFT_ROWS, c)


def _fft1(f1, x3):
    n1, n2, c = x3.shape
    return pl.pallas_call(
        _fft1_kernel,
        grid=(n2 // FFT_STEP_ROWS,),
        in_specs=[pl.BlockSpec(f1.shape, lambda j: (0, 0)),
                  pl.BlockSpec((n1, FFT_STEP_ROWS, c), lambda j: (0, j, 0))],
        out_specs=pl.BlockSpec((2 * n1, FFT_STEP_ROWS, c), lambda j: (0, j, 0)),
        out_shape=jax.ShapeDtypeStruct((2 * n1, n2, c), F32),
        compiler_params=pltpu.CompilerParams(
            dimension_semantics=("arbitrary",), vmem_limit_bytes=VMEM_LIMIT),
        name="fft_stage1",
    )(f1, x3)


def _fft2_kernel(y_ref, twc_ref, tws_ref, f2_ref, ch_ref, wf_ref, o_ref, *, scale):
    f2 = f2_ref[...].astype(BF16)
    ch = ch_ref[...].astype(BF16)
    _, rows, n2, c = y_ref.shape
    ps, qs = [], []
    for r in range(rows):
        yr = y_ref[0, r]
        yi = y_ref[1, r]
        tc = twc_ref[r]
        ts = tws_ref[r]
        zr = yr * tc + yi * ts
        zi = yi * tc - yr * ts
        z = jnp.concatenate([zr, zi], axis=0).astype(BF16)
        pq = jnp.dot(f2, z, preferred_element_type=F32)
        ps.append(pq[:n2].astype(BF16))
        qs.append(pq[n2:].astype(BF16))
    p = jnp.concatenate(ps, axis=0)
    q = jnp.concatenate(qs, axis=0)
    fs = []
    for g in range(c // GROUP):
        cols = slice(g * GROUP, (g + 1) * GROUP)
        pq_g = jnp.concatenate([p[:, cols], q[:, cols]], axis=1)
        fs.append(jnp.dot(pq_g, ch, preferred_element_type=F32))
    f = (jnp.concatenate(fs, axis=1) * scale).astype(BF16)
    ya = jnp.dot(f, wf_ref[...], preferred_element_type=F32)
    for r in range(rows):
        o_ref[:, r, :] = ya[r * n2:(r + 1) * n2]


def _fft2(y4, twc, tws, f2, ch, wf_bf, layer, scale):
    _, n1, n2, c = y4.shape
    const = lambda shape: pl.BlockSpec(shape, lambda k: tuple(0 for _ in shape))
    return pl.pallas_call(
        functools.partial(_fft2_kernel, scale=scale),
        grid=(n1 // FFT_ROWS,),
        in_specs=[
            pl.BlockSpec((2, FFT_ROWS, n2, c), lambda k: (0, k, 0, 0)),
            pl.BlockSpec((FFT_ROWS, n2, 1), lambda k: (k, 0, 0)),
            pl.BlockSpec((FFT_ROWS, n2, 1), lambda k: (k, 0, 0)),
            const((2 * n2, 2 * n2)),
            const(ch.shape),
            pl.BlockSpec((None, c, c), lambda k: (layer, 0, 0)),
        ],
        out_specs=pl.BlockSpec((n2, FFT_ROWS, c), lambda k: (0, k, 0)),
        out_shape=jax.ShapeDtypeStruct((n2, n1, c), F32),
        compiler_params=pltpu.CompilerParams(
            dimension_semantics=("arbitrary",), vmem_limit_bytes=VMEM_LIMIT),
        name="fft_stage2",
    )(y4, twc, tws, f2, ch, wf_bf)


def _fourier_branch(a_in, wf_bf, layer, consts):
    s, c = a_in.shape
    n2 = FFT_N2
    n1 = s // n2
    f1, f2, twc, tws, ch = consts
    y = _fft1(f1, a_in.reshape(n1, n2, c))
    scale = 1.0 / math.sqrt(s * GROUP)
    out = _fft2(y.reshape(2, n1, n2, c), twc, tws, f2, ch, wf_bf, layer, scale)
    return out.reshape(s, c)


def _t5_bucket(rel):
    nb = N_BUCKETS // 2
    ret = (rel > 0).astype(jnp.int32) * nb
    n = jnp.abs(rel)
    max_exact = nb // 2
    nf = jnp.maximum(n, 1).astype(jnp.float32)
    large = max_exact + (jnp.log(nf / max_exact) / math.log(MAX_DISTANCE / max_exact)
                         * (nb - max_exact)).astype(jnp.int32)
    large = jnp.minimum(large, nb - 1)
    return ret + jnp.where(n < max_exact, n, large)


def _bias_kernel(rb_ref, bk_ref, o_ref):
    h = pl.program_id(0)
    half = N_BUCKETS // 2
    t = bk_ref.shape[1]
    blk = MAX_DISTANCE
    for tile, d in enumerate((-1, 0, 1)):
        for kb in range(t // blk):
            for qb in range(t // blk):
                rel_lo = d * t + kb * blk - (qb * blk + blk - 1)
                rel_hi = d * t + kb * blk + blk - 1 - qb * blk
                rows, cols = slice(kb * blk, (kb + 1) * blk), slice(qb * blk, (qb + 1) * blk)
                if rel_hi <= -blk:
                    val = jnp.full((blk, blk), rb_ref[half - 1, h], F32)
                elif rel_lo >= blk:
                    val = jnp.full((blk, blk), rb_ref[N_BUCKETS - 1, h], F32)
                else:
                    bk = bk_ref[tile, rows, cols]
                    val = jnp.zeros(bk.shape, F32)
                    for b in range(0 if rel_lo <= 0 else half, half if rel_hi < 0 else N_BUCKETS):
                        val = jnp.where(bk == b, rb_ref[b, h], val)
                o_ref[0, tile, rows, cols] = val * LOG2E
    o_ref[0, ZERO_BIAS_TILE] = jnp.zeros((t, t), F32)


def _bias_tiles(rel_bias, t):
    kk = jnp.arange(t, dtype=jnp.int32)[:, None]
    rr = jnp.arange(t, dtype=jnp.int32)[None, :]
    rel = jnp.stack([d * t + kk - rr for d in (-1, 0, 1)])
    bucket = _t5_bucket(rel)
    return pl.pallas_call(
        _bias_kernel,
        grid=(N_HEADS,),
        in_specs=[pl.BlockSpec(memory_space=pltpu.SMEM),
                  pl.BlockSpec(bucket.shape, lambda h: (0, 0, 0))],
        out_specs=pl.BlockSpec((1, N_BIAS_TILES, t, t), lambda h: (h, 0, 0, 0)),
        out_shape=jax.ShapeDtypeStruct((N_HEADS, N_BIAS_TILES, t, t), F32),
        compiler_params=pltpu.CompilerParams(
            dimension_semantics=("arbitrary",), vmem_limit_bytes=VMEM_LIMIT),
        name="bias_tiles",
    )(rel_bias, bucket)


def _attn_kernel(bnd_ref, lq_ref, qt_ref, k_ref, vt_ref, bias_ref, gate_ref, gain_ref, o_ref,
                 q2_scr, m_scr, l_scr, acc_scr, kabs_scr, mreg_scr, pa_scr, pb_scr,
                 *, lambda_init, n_tiles):
    tq = qt_ref.shape[1] * qt_ref.shape[3]
    t = vt_ref.shape[-1]
    nqb = tq // t
    h = pl.program_id(0)
    i = pl.program_id(1)

    @pl.when(i == 0)
    def _():
        ka = jnp.max(jnp.abs(k_ref[...].astype(F32)), axis=0, keepdims=True)
        kabs_scr[...] = jnp.broadcast_to(ka, (8, HEAD_DIM)).T

    qt = jnp.concatenate([qt_ref[0, r] for r in range(qt_ref.shape[1])], axis=1)
    row = lax.broadcasted_iota(jnp.int32, qt.shape, 0)
    zero = jnp.zeros_like(qt)
    q2_scr[:, :tq] = jnp.where(row < HALF_DIM, qt, zero)
    q2_scr[:, tq:] = jnp.where(row >= HALF_DIM, qt, zero)
    l_scr[...] = jnp.zeros(l_scr.shape, F32)
    acc_scr[...] = jnp.zeros(acc_scr.shape, F32)

    c_left = bias_ref[0, 0, 0:1, t - 1:t]
    c_right = bias_ref[0, 2, t - 1:t, 0:1]

    wq = jnp.abs(qt.astype(F32)) * kabs_scr[:, 0:1]
    hb = jnp.concatenate([jnp.sum(wq[:HALF_DIM], axis=0, keepdims=True),
                          jnp.sum(wq[HALF_DIM:], axis=0, keepdims=True)], axis=1)
    b_max = bnd_ref[h, 0]
    b_min = bnd_ref[h, 1]
    fixed_ok = 2.0 * jnp.max(hb) + (b_max - b_min) <= EXP2_SPAN

    def block_relation(j, col0):
        e = j - (nqb * i + (col0 % tq) // t)
        tile_idx = jnp.where(jnp.abs(e) <= 1, e + 1, ZERO_BIAS_TILE)
        region = jnp.where(e < -1, 0, jnp.where(e > 1, 2, 1))
        return tile_idx, region

    def step(j_p, p_out, j_v, p_in, with_bias):
        if j_p is not None:
            kt = k_ref[pl.ds(pl.multiple_of(j_p * t, t), t), :]
        if j_v is not None:
            vt = vt_ref[0, j_v]
        for c0 in range(0, 2 * tq, MXU_COLS):
            cs = slice(c0, c0 + MXU_COLS)
            if j_p is not None:
                tile_idx, region = block_relation(j_p, c0)
                s = jnp.dot(kt, q2_scr[:, cs], preferred_element_type=F32)
                if with_bias:
                    b0 = c0 % t
                    s = s + bias_ref[0, tile_idx, :, b0:b0 + MXU_COLS]
                p = jnp.exp2(s - mreg_scr[region, :, cs])
                l_scr[:, cs] += jnp.sum(p, axis=0, keepdims=True)
                p_out[:, cs] = p.astype(BF16)
            if j_v is not None:
                acc_scr[:, cs] += jnp.dot(vt, p_in[:, cs], preferred_element_type=F32)

    def near_band(j):
        return jnp.logical_and(j >= nqb * i - 1, j <= nqb * i + nqb)

    def step_pair(j, last):
        def run(with_bias):
            step(j + 1, pb_scr, j, pa_scr, with_bias)
            step(None if last else j + 2, pa_scr, j + 1, pb_scr, with_bias)

        near = near_band(j + 1) if last else jnp.logical_or(near_band(j + 1), near_band(j + 2))
        lax.cond(near, lambda: run(True), lambda: run(False))

    @pl.when(fixed_ok)
    def _():
        m0 = hb + b_max
        mreg_scr[0] = m0 - c_left
        mreg_scr[1] = m0
        mreg_scr[2] = m0 - c_right

        def pair_body(jj, carry):
            step_pair(2 * jj, last=False)
            return carry

        step(0, pa_scr, None, None, True)
        lax.fori_loop(0, n_tiles // 2 - 1, pair_body, 0)
        step_pair(n_tiles - 2, last=True)

    def online_step(j):
        kt = k_ref[pl.ds(pl.multiple_of(j * t, t), t), :]
        vt = vt_ref[0, j]
        for c0 in range(0, 2 * tq, t):
            cs = slice(c0, c0 + t)
            tile_idx, region = block_relation(j, c0)
            c_far = jnp.where(region == 0, c_left, jnp.where(region == 2, c_right, 0.0))
            s = (jnp.dot(kt, q2_scr[:, cs], preferred_element_type=F32)
                 + (bias_ref[0, tile_idx] + c_far))
            m_old = m_scr[:, cs]
            m_new = jnp.maximum(m_old, jnp.max(s, axis=0, keepdims=True))
            p = jnp.exp2(s - m_new)
            alpha = jnp.exp2(m_old - m_new)
            l_scr[:, cs] = alpha * l_scr[:, cs] + jnp.sum(p, axis=0, keepdims=True)
            pv = jnp.dot(vt, p.astype(BF16), preferred_element_type=F32)
            acc_scr[:, cs] = alpha * acc_scr[:, cs] + pv
            m_scr[:, cs] = m_new

    @pl.when(jnp.logical_not(fixed_ok))
    def _():
        def body(j, carry):
            online_step(j)
            return carry

        m_scr[...] = jnp.full(m_scr.shape, NEG_INIT, F32)
        lax.fori_loop(0, n_tiles, body, 0)

    lq = lq_ref[...]
    lam = (jnp.exp(jnp.sum(lq[0:1] * lq[1:2], axis=-1, keepdims=True))
           - jnp.exp(jnp.sum(lq[2:3] * lq[3:4], axis=-1, keepdims=True)) + lambda_init)
    accn = acc_scr[...] * (1.0 / l_scr[...])
    o_t = accn[:, :tq] - lam * accn[:, tq:]
    ms = jnp.mean(o_t * o_t, axis=0, keepdims=True)
    o = (o_t * lax.rsqrt(ms + EPS)).T
    y = (o * gain_ref[...]) * (1.0 - lambda_init)
    o_ref[...] = (y * _silu(gate_ref[...].astype(F32))).astype(BF16)


def _attention(bnd, lq, qt4, proj, vt4, bias, gain, lambda_init):
    s = proj.shape[0]
    tq = TQ_ATT
    t = vt4.shape[-1]
    assert t == bias.shape[-1] and tq % t == 0
    n_tiles = s // t
    kblk = OFF_K // HEAD_DIM
    gblk = OFF_B_GATE // HEAD_DIM
    kern = functools.partial(_attn_kernel, lambda_init=lambda_init, n_tiles=n_tiles)
    return pl.pallas_call(
        kern,
        grid=(N_HEADS, s // tq),
        in_specs=[
            pl.BlockSpec(memory_space=pltpu.SMEM),
            pl.BlockSpec((4, HALF_DIM), lambda h, i: (0, 0)),
            pl.BlockSpec((1, tq // t, HEAD_DIM, t), lambda h, i: (h, i, 0, 0)),
            pl.BlockSpec((s, HEAD_DIM), lambda h, i: (0, kblk + h)),
            pl.BlockSpec((1, n_tiles, HEAD_DIM, t), lambda h, i: (h, 0, 0, 0)),
            pl.BlockSpec((1, N_BIAS_TILES, t, t), lambda h, i: (h, 0, 0, 0)),
            pl.BlockSpec((tq, HEAD_DIM), lambda h, i: (i, gblk + h)),
            pl.BlockSpec((1, HEAD_DIM), lambda h, i: (0, h)),
        ],
        out_specs=pl.BlockSpec((tq, HEAD_DIM), lambda h, i: (i, h)),
        out_shape=jax.ShapeDtypeStruct((s, D_DIFF), BF16),
        scratch_shapes=[
            pltpu.VMEM((HEAD_DIM, 2 * tq), BF16),
            pltpu.VMEM((1, 2 * tq), F32),
            pltpu.VMEM((1, 2 * tq), F32),
            pltpu.VMEM((HEAD_DIM, 2 * tq), F32),
            pltpu.VMEM((HEAD_DIM, 8), F32),
            pltpu.VMEM((3, 1, 2 * tq), F32),
            pltpu.VMEM((t, 2 * tq), BF16),
            pltpu.VMEM((t, 2 * tq), BF16),
        ],
        compiler_params=pltpu.CompilerParams(
            dimension_semantics=("arbitrary", "arbitrary"), vmem_limit_bytes=VMEM_LIMIT),
        name="diff_attention",
    )(bnd, lq, qt4, proj, vt4, bias, proj, gain)


def _outproj_kernel(ya_ref, ag_ref, yb_ref, u_ref, vsg_ref, cg_ref, x_ref, wout_ref, ws_ref,
                    bs_ref, vgain_ref, pgain_ref, o_ref, y_scr):
    tm = x_ref.shape[0]
    y_scr[:, 0:D_FOURIER] = (ya_ref[...] * _silu(ag_ref[...].astype(F32))).astype(BF16)
    y_scr[:, D_FOURIER:D_FOURIER + D_DIFF] = yb_ref[...]
    for g in range(D_GMLP // GROUP):
        cols = slice(g * GROUP, (g + 1) * GROUP)
        v = vsg_ref[:, cols].astype(F32)
        ms = jnp.mean(v * v, axis=-1, keepdims=True)
        vn = ((v * lax.rsqrt(ms + EPS)) * vgain_ref[:, cols]).astype(BF16)
        w = ws_ref[g]
        b = bs_ref[g]
        for c in range(tm // CHUNK):
            rows = slice(c * CHUNK, (c + 1) * CHUNK)
            mixed = jnp.dot(w, vn[rows], preferred_element_type=F32) + b
            gate = cg_ref[rows, cols].astype(F32)
            yc = (u_ref[rows, cols].astype(F32) * mixed) * _silu(gate)
            y_scr[rows, D_FOURIER + D_DIFF + g * GROUP:D_FOURIER + D_DIFF + (g + 1) * GROUP] = (
                yc.astype(BF16))
    y = jnp.dot(y_scr[...], wout_ref[...], preferred_element_type=F32)
    ms = jnp.mean(y * y, axis=-1, keepdims=True)
    o_ref[...] = x_ref[...] + (y * lax.rsqrt(ms + EPS)) * pgain_ref[...]


def _outproj(ya, yb, proj, x2, wout_bf, ws_bf, layer, bs3, vgain, pgain):
    s, d = x2.shape
    tm = TM_OUT
    ablk = OFF_A_GATE // D_FOURIER
    ublk, vblk, cblk = OFF_U // D_GMLP, OFF_VSG // D_GMLP, OFF_C_GATE // D_GMLP
    const = lambda shape: pl.BlockSpec(shape, lambda i: tuple(0 for _ in shape))
    return pl.pallas_call(
        _outproj_kernel,
        grid=(s // tm,),
        in_specs=[
            pl.BlockSpec((tm, D_FOURIER), lambda i: (i, 0)),
            pl.BlockSpec((tm, D_FOURIER), lambda i: (i, ablk)),
            pl.BlockSpec((tm, D_DIFF), lambda i: (i, 0)),
            pl.BlockSpec((tm, D_GMLP), lambda i: (i, ublk)),
            pl.BlockSpec((tm, D_GMLP), lambda i: (i, vblk)),
            pl.BlockSpec((tm, D_GMLP), lambda i: (i, cblk)),
            pl.BlockSpec((tm, d), lambda i: (i, 0)),
            pl.BlockSpec((None, d, d), lambda i: (layer, 0, 0)),
            pl.BlockSpec((None,) + ws_bf.shape[1:], lambda i: (layer, 0, 0, 0)),
            const(bs3.shape),
            const((1, D_GMLP)),
            const((1, d)),
        ],
        out_specs=pl.BlockSpec((tm, d), lambda i: (i, 0)),
        out_shape=jax.ShapeDtypeStruct((s, d), F32),
        scratch_shapes=[pltpu.VMEM((tm, d), BF16)],
        compiler_params=pltpu.CompilerParams(
            dimension_semantics=("arbitrary",), vmem_limit_bytes=VMEM_LIMIT),
        name="outproj",
    )(ya, proj, yb, proj, proj, proj, x2, wout_bf, ws_bf, bs3, vgain, pgain)


def kernel(x, w_in, pre_gain, post_gain, w_fourier, lambda_qk, diff_out_gain, sg_v_gain,
           w_spatial, b_spatial, w_out, rel_bias):
    b, s, d = x.shape
    depth = w_in.shape[0]
    assert b == 1 and w_in.shape[2] == D_IN
    x2 = x.reshape(s, d)

    q_scale = (HALF_DIM ** -0.5) * LOG2E
    colscale = jnp.ones((1, D_IN), F32).at[:, OFF_Q:OFF_Q + D_DIFF].set(q_scale)
    bias = _bias_tiles(rel_bias, T_ATT)
    bias_bounds = jnp.stack([jnp.max(rel_bias, axis=0), jnp.min(rel_bias, axis=0)], axis=1) * LOG2E
    consts = _dft_consts(s // FFT_N2, FFT_N2, GROUP)

    w_in_bf, w_out_bf = w_in.astype(BF16), w_out.astype(BF16)
    w_fourier_bf, w_spatial_bf = w_fourier.astype(BF16), w_spatial.astype(BF16)

    for l in range(depth):
        lambda_init = 0.8 - 0.6 * math.exp(-0.3 * l)
        proj, a_in, qt4, vt4 = _inproj(x2, pre_gain[l].reshape(1, d), w_in_bf, l, colscale)
        ya = _fourier_branch(a_in, w_fourier_bf, l, consts)
        yb = _attention(bias_bounds, lambda_qk[l], qt4, proj, vt4, bias,
                        diff_out_gain[l].reshape(1, D_DIFF), lambda_init)
        x2 = _outproj(ya, yb, proj, x2, w_out_bf, w_spatial_bf, l,
                      b_spatial[l].reshape(D_GMLP // GROUP, CHUNK, 1),
                      sg_v_gain[l].reshape(1, D_GMLP), post_gain[l].reshape(1, d))
    return x2.reshape(b, s, d)
```

```python
import functools
import math

import numpy as np
import jax
import jax.numpy as jnp
from jax import lax
from jax.experimental import pallas as pl
from jax.experimental.pallas import tpu as pltpu

F32 = jnp.float32
BF16 = jnp.bfloat16

EPS = 1e-6
LOG2E = math.log2(math.e)
NEG_INIT = -1e30
EXP2_SPAN = 120.0

D_FOURIER = 512
D_DIFF = 1024
D_GMLP = 512
GROUP = 128
HEAD_DIM = 128
HALF_DIM = 64
N_HEADS = 8
CHUNK = 128
N_BUCKETS = 32
MAX_DISTANCE = 128
OFF_A_IN, OFF_A_GATE, OFF_Q, OFF_K, OFF_V, OFF_B_GATE, OFF_U, OFF_VSG, OFF_C_GATE = (
    0, 512, 1024, 2048, 3072, 4096, 5120, 5632, 6144)
D_IN = 6656

MXU_COLS = 256
TM_IN = 512
TN_IN = 3328
TM_OUT = 512
T_ATT = 512
TQ_ATT = 2048
N_BIAS_TILES = 4
ZERO_BIAS_TILE = 3
FFT_N2 = 128
FFT_ROWS = 8
FFT_STEP_ROWS = 32
VMEM_LIMIT = 56 * 1024 * 1024
VMEM_LIMIT_INPROJ = 62 * 1024 * 1024
assert TM_IN % T_ATT == 0


def _silu(x):
    return x * (1.0 / (1.0 + jnp.exp(-x)))


def _inproj_kernel(x_ref, g_ref, w_ref, cs_ref, proj_ref, ain_ref, qt_ref, vt_ref, a_scr):
    j = pl.program_id(1)

    @pl.when(j == 0)
    def _():
        x = x_ref[...]
        ms = jnp.mean(x * x, axis=-1, keepdims=True)
        a_scr[...] = ((x * lax.rsqrt(ms + EPS)) * g_ref[...]).astype(BF16)

    acc = jnp.dot(a_scr[...], w_ref[...], preferred_element_type=F32)
    scaled = acc * cs_ref[...]
    proj_ref[...] = scaled.astype(BF16)

    @pl.when(j == 0)
    def _():
        ain_ref[...] = acc[:, :D_FOURIER]

    tn = proj_ref.shape[1]
    for jj in range(D_IN // tn):
        heads = [(dst, (c - off) // HEAD_DIM, c - jj * tn)
                 for dst, off in ((qt_ref, OFF_Q), (vt_ref, OFF_V))
                 for c in range(off, off + D_DIFF, HEAD_DIM)
                 if jj * tn <= c < (jj + 1) * tn]
        if heads:
            @pl.when(j == jj)
            def _(heads=heads):
                for dst, h, c0 in heads:
                    for r in range(dst.shape[1]):
                        rows = slice(r * T_ATT, (r + 1) * T_ATT)
                        dst[h, r] = scaled[rows, c0:c0 + HEAD_DIM].T.astype(BF16)


def _inproj(x2, gain, w_bf, layer, colscale):
    s, d = x2.shape
    n = w_bf.shape[2]
    grid = (s // TM_IN, n // TN_IN)
    head_tiles = pl.BlockSpec((N_HEADS, TM_IN // T_ATT, HEAD_DIM, T_ATT),
                              lambda i, j: (0, i, 0, 0))
    return pl.pallas_call(
        _inproj_kernel,
        grid=grid,
        in_specs=[
            pl.BlockSpec((TM_IN, d), lambda i, j: (i, 0)),
            pl.BlockSpec((1, d), lambda i, j: (0, 0)),
            pl.BlockSpec((None, d, TN_IN), lambda i, j: (layer, 0, j)),
            pl.BlockSpec((1, TN_IN), lambda i, j: (0, j)),
        ],
        out_specs=[
            pl.BlockSpec((TM_IN, TN_IN), lambda i, j: (i, j)),
            pl.BlockSpec((TM_IN, D_FOURIER), lambda i, j: (i, 0)),
            head_tiles,
            head_tiles,
        ],
        out_shape=[
            jax.ShapeDtypeStruct((s, n), BF16),
            jax.ShapeDtypeStruct((s, D_FOURIER), F32),
            jax.ShapeDtypeStruct((N_HEADS, s // T_ATT, HEAD_DIM, T_ATT), BF16),
            jax.ShapeDtypeStruct((N_HEADS, s // T_ATT, HEAD_DIM, T_ATT), BF16),
        ],
        scratch_shapes=[pltpu.VMEM((TM_IN, d), BF16)],
        compiler_params=pltpu.CompilerParams(
            dimension_semantics=("arbitrary", "arbitrary"),
            vmem_limit_bytes=VMEM_LIMIT_INPROJ),
        name="inproj",
    )(x2, gain, w_bf, colscale)


def _dft_consts(n1, n2, c):
    def cs(n):
        idx = np.arange(n)
        ang = 2.0 * np.pi * ((idx[:, None] * idx[None, :]) % n) / n
        return np.cos(ang), np.sin(ang)

    c1, s1 = cs(n1)
    c2, s2 = cs(n2)
    cc, sc = cs(c)
    f1 = np.kron(np.concatenate([c1, -s1], axis=0), np.eye(FFT_ROWS))
    f2 = np.block([[c2, s2], [s2, -c2]])
    n = n1 * n2
    k1 = np.arange(n1)[:, None]
    m2 = np.arange(n2)[None, :]
    ang = 2.0 * np.pi * ((k1 * m2) % n) / n
    tw_c = np.cos(ang)[:, :, None]
    tw_s = np.sin(ang)[:, :, None]
    ch = np.concatenate([cc, -sc], axis=0)
    f32 = lambda a: jnp.asarray(a.astype(np.float32))
    return f32(f1), f32(f2), f32(tw_c), f32(tw_s), f32(ch)


def _fft1_kernel(f1_ref, x_ref, y_ref):
    n1, rows, c = x_ref.shape
    f1 = f1_ref[...].astype(BF16)
    for r0 in range(0, rows, FFT_ROWS):
        blk = slice(r0, r0 + FFT_ROWS)
        x = x_ref[:, blk, :].reshape(n1 * FFT_ROWS, c).astype(BF16)
        y = jnp.dot(f1, x, preferred_element_type=F32)
        y_ref[:, blk, :] = y.reshape(y_ref.shape[0], FFT_ROWS, c)


def _fft1(f1, x3):
    n1, n2, c = x3.shape
    return pl.pallas_call(
        _fft1_kernel,
        grid=(n2 // FFT_STEP_ROWS,),
        in_specs=[pl.BlockSpec(f1.shape, lambda j: (0, 0)),
                  pl.BlockSpec((n1, FFT_STEP_ROWS, c), lambda j: (0, j, 0))],
        out_specs=pl.BlockSpec((2 * n1, FFT_STEP_ROWS, c), lambda j: (0, j, 0)),
        out_shape=jax.ShapeDtypeStruct((2 * n1, n2, c), F32),
        compiler_params=pltpu.CompilerParams(
            dimension_semantics=("arbitrary",), vmem_limit_bytes=VMEM_LIMIT),
        name="fft_stage1",
    )(f1, x3)


def _fft2_kernel(y_ref, twc_ref, tws_ref, f2_ref, ch_ref, wf_ref, o_ref, *, scale):
    f2 = f2_ref[...].astype(BF16)
    ch = ch_ref[...].astype(BF16)
    _, rows, n2, c = y_ref.shape
    ps, qs = [], []
    for r in range(rows):
        yr = y_ref[0, r]
        yi = y_ref[1, r]
        tc = twc_ref[r]
        ts = tws_ref[r]
        zr = yr * tc + yi * ts
        zi = yi * tc - yr * ts
        z = jnp.concatenate([zr, zi], axis=0).astype(BF16)
        pq = jnp.dot(f2, z, preferred_element_type=F32)
        ps.append(pq[:n2].astype(BF16))
        qs.append(pq[n2:].astype(BF16))
    p = jnp.concatenate(ps, axis=0)
    q = jnp.concatenate(qs, axis=0)
    fs = []
    for g in range(c // GROUP):
        cols = slice(g * GROUP, (g + 1) * GROUP)
        pq_g = jnp.concatenate([p[:, cols], q[:, cols]], axis=1)
        fs.append(jnp.dot(pq_g, ch, preferred_element_type=F32))
    f = (jnp.concatenate(fs, axis=1) * scale).astype(BF16)
    ya = jnp.dot(f, wf_ref[...], preferred_element_type=F32)
    for r in range(rows):
        o_ref[:, r, :] = ya[r * n2:(r + 1) * n2]


def _fft2(y4, twc, tws, f2, ch, wf_bf, layer, scale):
    _, n1, n2, c = y4.shape
    const = lambda shape: pl.BlockSpec(shape, lambda k: tuple(0 for _ in shape))
    return pl.pallas_call(
        functools.partial(_fft2_kernel, scale=scale),
        grid=(n1 // FFT_ROWS,),
        in_specs=[
            pl.BlockSpec((2, FFT_ROWS, n2, c), lambda k: (0, k, 0, 0)),
            pl.BlockSpec((FFT_ROWS, n2, 1), lambda k: (k, 0, 0)),
            pl.BlockSpec((FFT_ROWS, n2, 1), lambda k: (k, 0, 0)),
            const((2 * n2, 2 * n2)),
            const(ch.shape),
            pl.BlockSpec((None, c, c), lambda k: (layer, 0, 0)),
        ],
        out_specs=pl.BlockSpec((n2, FFT_ROWS, c), lambda k: (0, k, 0)),
        out_shape=jax.ShapeDtypeStruct((n2, n1, c), F32),
        compiler_params=pltpu.CompilerParams(
            dimension_semantics=("arbitrary",), vmem_limit_bytes=VMEM_LIMIT),
        name="fft_stage2",
    )(y4, twc, tws, f2, ch, wf_bf)


def _fourier_branch(a_in, wf_bf, layer, consts):
    s, c = a_in.shape
    n2 = FFT_N2
    n1 = s // n2
    f1, f2, twc, tws, ch = consts
    y = _fft1(f1, a_in.reshape(n1, n2, c))
    scale = 1.0 / math.sqrt(s * GROUP)
    out = _fft2(y.reshape(2, n1, n2, c), twc, tws, f2, ch, wf_bf, layer, scale)
    return out.reshape(s, c)


def _t5_bucket(rel):
    nb = N_BUCKETS // 2
    ret = (rel > 0).astype(jnp.int32) * nb
    n = jnp.abs(rel)
    max_exact = nb // 2
    nf = jnp.maximum(n, 1).astype(jnp.float32)
    large = max_exact + (jnp.log(nf / max_exact) / math.log(MAX_DISTANCE / max_exact)
                         * (nb - max_exact)).astype(jnp.int32)
    large = jnp.minimum(large, nb - 1)
    return ret + jnp.where(n < max_exact, n, large)


def _bias_kernel(rb_ref, bk_ref, o_ref):
    h = pl.program_id(0)
    half = N_BUCKETS // 2
    t = bk_ref.shape[1]
    blk = MAX_DISTANCE
    for tile, d in enumerate((-1, 0, 1)):
        for kb in range(t // blk):
            for qb in range(t // blk):
                rel_lo = d * t + kb * blk - (qb * blk + blk - 1)
                rel_hi = d * t + kb * blk + blk - 1 - qb * blk
                rows, cols = slice(kb * blk, (kb + 1) * blk), slice(qb * blk, (qb + 1) * blk)
                if rel_hi <= -blk:
                    val = jnp.full((blk, blk), rb_ref[half - 1, h], F32)
                elif rel_lo >= blk:
                    val = jnp.full((blk, blk), rb_ref[N_BUCKETS - 1, h], F32)
                else:
                    bk = bk_ref[tile, rows, cols]
                    val = jnp.zeros(bk.shape, F32)
                    for b in range(0 if rel_lo <= 0 else half, half if rel_hi < 0 else N_BUCKETS):
                        val = jnp.where(bk == b, rb_ref[b, h], val)
                o_ref[0, tile, rows, cols] = val * LOG2E
    o_ref[0, ZERO_BIAS_TILE] = jnp.zeros((t, t), F32)


def _bias_tiles(rel_bias, t):
    kk = jnp.arange(t, dtype=jnp.int32)[:, None]
    rr = jnp.arange(t, dtype=jnp.int32)[None, :]
    rel = jnp.stack([d * t + kk - rr for d in (-1, 0, 1)])
    bucket = _t5_bucket(rel)
    return pl.pallas_call(
        _bias_kernel,
        grid=(N_HEADS,),
        in_specs=[pl.BlockSpec(memory_space=pltpu.SMEM),
                  pl.BlockSpec(bucket.shape, lambda h: (0, 0, 0))],
        out_specs=pl.BlockSpec((1, N_BIAS_TILES, t, t), lambda h: (h, 0, 0, 0)),
        out_shape=jax.ShapeDtypeStruct((N_HEADS, N_BIAS_TILES, t, t), F32),
        compiler_params=pltpu.CompilerParams(
            dimension_semantics=("arbitrary",), vmem_limit_bytes=VMEM_LIMIT),
        name="bias_tiles",
    )(rel_bias, bucket)


def _attn_kernel(bnd_ref, lq_ref, qt_ref, k_ref, vt_ref, bias_ref, gate_ref, gain_ref, o_ref,
                 q2_scr, m_scr, l_scr, acc_scr, kabs_scr, mreg_scr, pa_scr, pb_scr,
                 *, lambda_init, n_tiles):
    tq = qt_ref.shape[1] * qt_ref.shape[3]
    t = vt_ref.shape[-1]
    nqb = tq // t
    h = pl.program_id(0)
    i = pl.program_id(1)

    @pl.when(i == 0)
    def _():
        ka = jnp.max(jnp.abs(k_ref[...].astype(F32)), axis=0, keepdims=True)
        kabs_scr[...] = jnp.broadcast_to(ka, (8, HEAD_DIM)).T

    qt = jnp.concatenate([qt_ref[0, r] for r in range(qt_ref.shape[1])], axis=1)
    row = lax.broadcasted_iota(jnp.int32, qt.shape, 0)
    zero = jnp.zeros_like(qt)
    q2_scr[:, :tq] = jnp.where(row < HALF_DIM, qt, zero)
    q2_scr[:, tq:] = jnp.where(row >= HALF_DIM, qt, zero)
    l_scr[...] = jnp.zeros(l_scr.shape, F32)
    acc_scr[...] = jnp.zeros(acc_scr.shape, F32)

    c_left = bias_ref[0, 0, 0:1, t - 1:t]
    c_right = bias_ref[0, 2, t - 1:t, 0:1]

    wq = jnp.abs(qt.astype(F32)) * kabs_scr[:, 0:1]
    hb = jnp.concatenate([jnp.sum(wq[:HALF_DIM], axis=0, keepdims=True),
                          jnp.sum(wq[HALF_DIM:], axis=0, keepdims=True)], axis=1)
    b_max = bnd_ref[h, 0]
    b_min = bnd_ref[h, 1]
    fixed_ok = 2.0 * jnp.max(hb) + (b_max - b_min) <= EXP2_SPAN

    def block_relation(j, col0):
        e = j - (nqb * i + (col0 % tq) // t)
        tile_idx = jnp.where(jnp.abs(e) <= 1, e + 1, ZERO_BIAS_TILE)
        region = jnp.where(e < -1, 0, jnp.where(e > 1, 2, 1))
        return tile_idx, region

    def step(j_p, p_out, j_v, p_in, with_bias):
        if j_p is not None:
            kt = k_ref[pl.ds(pl.multiple_of(j_p * t, t), t), :]
        if j_v is not None:
            vt = vt_ref[0, j_v]
        for c0 in range(0, 2 * tq, MXU_COLS):
            cs = slice(c0, c0 + MXU_COLS)
            if j_p is not None:
                tile_idx, region = block_relation(j_p, c0)
                s = jnp.dot(kt, q2_scr[:, cs], preferred_element_type=F32)
                if with_bias:
                    b0 = c0 % t
                    s = s + bias_ref[0, tile_idx, :, b0:b0 + MXU_COLS]
                p = jnp.exp2(s - mreg_scr[region, :, cs])
                l_scr[:, cs] += jnp.sum(p, axis=0, keepdims=True)
                p_out[:, cs] = p.astype(BF16)
            if j_v is not None:
                acc_scr[:, cs] += jnp.dot(vt, p_in[:, cs], preferred_element_type=F32)

    def near_band(j):
        return jnp.logical_and(j >= nqb * i - 1, j <= nqb * i + nqb)

    def step_pair(j, last):
        def run(with_bias):
            step(j + 1, pb_scr, j, pa_scr, with_bias)
            step(None if last else j + 2, pa_scr, j + 1, pb_scr, with_bias)

        near = near_band(j + 1) if last else jnp.logical_or(near_band(j + 1), near_band(j + 2))
        lax.cond(near, lambda: run(True), lambda: run(False))

    @pl.when(fixed_ok)
    def _():
        m0 = hb + b_max
        mreg_scr[0] = m0 - c_left
        mreg_scr[1] = m0
        mreg_scr[2] = m0 - c_right

        def pair_body(jj, carry):
            step_pair(2 * jj, last=False)
            return carry

        step(0, pa_scr, None, None, True)
        lax.fori_loop(0, n_tiles // 2 - 1, pair_body, 0)
        step_pair(n_tiles - 2, last=True)

    def online_step(j):
        kt = k_ref[pl.ds(pl.multiple_of(j * t, t), t), :]
        vt = vt_ref[0, j]
        for c0 in range(0, 2 * tq, t):
            cs = slice(c0, c0 + t)
            tile_idx, region = block_relation(j, c0)
            c_far = jnp.where(region == 0, c_left, jnp.where(region == 2, c_right, 0.0))
            s = (jnp.dot(kt, q2_scr[:, cs], preferred_element_type=F32)
                 + (bias_ref[0, tile_idx] + c_far))
            m_old = m_scr[:, cs]
            m_new = jnp.maximum(m_old, jnp.max(s, axis=0, keepdims=True))
            p = jnp.exp2(s - m_new)
            alpha = jnp.exp2(m_old - m_new)
            l_scr[:, cs] = alpha * l_scr[:, cs] + jnp.sum(p, axis=0, keepdims=True)
            pv = jnp.dot(vt, p.astype(BF16), preferred_element_type=F32)
            acc_scr[:, cs] = alpha * acc_scr[:, cs] + pv
            m_scr[:, cs] = m_new

    @pl.when(jnp.logical_not(fixed_ok))
    def _():
        def body(j, carry):
            online_step(j)
            return carry

        m_scr[...] = jnp.full(m_scr.shape, NEG_INIT, F32)
        lax.fori_loop(0, n_tiles, body, 0)

    lq = lq_ref[...]
    lam = (jnp.exp(jnp.sum(lq[0:1] * lq[1:2], axis=-1, keepdims=True))
           - jnp.exp(jnp.sum(lq[2:3] * lq[3:4], axis=-1, keepdims=True)) + lambda_init)
    accn = acc_scr[...] * (1.0 / l_scr[...])
    o_t = accn[:, :tq] - lam * accn[:, tq:]
    ms = jnp.mean(o_t * o_t, axis=0, keepdims=True)
    o = (o_t * lax.rsqrt(ms + EPS)).T
    y = (o * gain_ref[...]) * (1.0 - lambda_init)
    o_ref[...] = (y * _silu(gate_ref[...].astype(F32))).astype(BF16)


def _attention(bnd, lq, qt4, proj, vt4, bias, gain, lambda_init):
    s = proj.shape[0]
    tq = TQ_ATT
    t = vt4.shape[-1]
    assert t == bias.shape[-1] and tq % t == 0
    n_tiles = s // t
    kblk = OFF_K // HEAD_DIM
    gblk = OFF_B_GATE // HEAD_DIM
    kern = functools.partial(_attn_kernel, lambda_init=lambda_init, n_tiles=n_tiles)
    return pl.pallas_call(
        kern,
        grid=(N_HEADS, s // tq),
        in_specs=[
            pl.BlockSpec(memory_space=pltpu.SMEM),
            pl.BlockSpec((4, HALF_DIM), lambda h, i: (0, 0)),
            pl.BlockSpec((1, tq // t, HEAD_DIM, t), lambda h, i: (h, i, 0, 0)),
            pl.BlockSpec((s, HEAD_DIM), lambda h, i: (0, kblk + h)),
            pl.BlockSpec((1, n_tiles, HEAD_DIM, t), lambda h, i: (h, 0, 0, 0)),
            pl.BlockSpec((1, N_BIAS_TILES, t, t), lambda h, i: (h, 0, 0, 0)),
            pl.BlockSpec((tq, HEAD_DIM), lambda h, i: (i, gblk + h)),
            pl.BlockSpec((1, HEAD_DIM), lambda h, i: (0, h)),
        ],
        out_specs=pl.BlockSpec((tq, HEAD_DIM), lambda h, i: (i, h)),
        out_shape=jax.ShapeDtypeStruct((s, D_DIFF), BF16),
        scratch_shapes=[
            pltpu.VMEM((HEAD_DIM, 2 * tq), BF16),
            pltpu.VMEM((1, 2 * tq), F32),
            pltpu.VMEM((1, 2 * tq), F32),
            pltpu.VMEM((HEAD_DIM, 2 * tq), F32),
            pltpu.VMEM((HEAD_DIM, 8), F32),
            pltpu.VMEM((3, 1, 2 * tq), F32),
            pltpu.VMEM((t, 2 * tq), BF16),
            pltpu.VMEM((t, 2 * tq), BF16),
        ],
        compiler_params=pltpu.CompilerParams(
            dimension_semantics=("arbitrary", "arbitrary"), vmem_limit_bytes=VMEM_LIMIT),
        name="diff_attention",
    )(bnd, lq, qt4, proj, vt4, bias, proj, gain)


def _outproj_kernel(ya_ref, ag_ref, yb_ref, u_ref, vsg_ref, cg_ref, x_ref, wout_ref, ws_ref,
                    bs_ref, vgain_ref, pgain_ref, o_ref, y_scr, wbf_scr):
    tm = x_ref.shape[0]

    @pl.when(pl.program_id(0) == 0)
    def _():
        wbf_scr[...] = wout_ref[...].astype(BF16)

    y_scr[:, 0:D_FOURIER] = (ya_ref[...] * _silu(ag_ref[...].astype(F32))).astype(BF16)
    y_scr[:, D_FOURIER:D_FOURIER + D_DIFF] = yb_ref[...]
    for g in range(D_GMLP // GROUP):
        cols = slice(g * GROUP, (g + 1) * GROUP)
        v = vsg_ref[:, cols].astype(F32)
        ms = jnp.mean(v * v, axis=-1, keepdims=True)
        vn = ((v * lax.rsqrt(ms + EPS)) * vgain_ref[:, cols]).astype(BF16)
        w = ws_ref[g]
        b = bs_ref[g]
        for c in range(tm // CHUNK):
            rows = slice(c * CHUNK, (c + 1) * CHUNK)
            mixed = jnp.dot(w, vn[rows], preferred_element_type=F32) + b
            gate = cg_ref[rows, cols].astype(F32)
            yc = (u_ref[rows, cols].astype(F32) * mixed) * _silu(gate)
            y_scr[rows, D_FOURIER + D_DIFF + g * GROUP:D_FOURIER + D_DIFF + (g + 1) * GROUP] = (
                yc.astype(BF16))
    y = jnp.dot(y_scr[...], wbf_scr[...], preferred_element_type=F32)
    ms = jnp.mean(y * y, axis=-1, keepdims=True)
    o_ref[...] = x_ref[...] + (y * lax.rsqrt(ms + EPS)) * pgain_ref[...]


def _outproj(ya, yb, proj, x2, wout_bf, ws_bf, layer, bs3, vgain, pgain):
    s, d = x2.shape
    tm = TM_OUT
    ablk = OFF_A_GATE // D_FOURIER
    ublk, vblk, cblk = OFF_U // D_GMLP, OFF_VSG // D_GMLP, OFF_C_GATE // D_GMLP
    const = lambda shape: pl.BlockSpec(shape, lambda i: tuple(0 for _ in shape))
    return pl.pallas_call(
        _outproj_kernel,
        grid=(s // tm,),
        in_specs=[
            pl.BlockSpec((tm, D_FOURIER), lambda i: (i, 0)),
            pl.BlockSpec((tm, D_FOURIER), lambda i: (i, ablk)),
            pl.BlockSpec((tm, D_DIFF), lambda i: (i, 0)),
            pl.BlockSpec((tm, D_GMLP), lambda i: (i, ublk)),
            pl.BlockSpec((tm, D_GMLP), lambda i: (i, vblk)),
            pl.BlockSpec((tm, D_GMLP), lambda i: (i, cblk)),
            pl.BlockSpec((tm, d), lambda i: (i, 0)),
            pl.BlockSpec((None, d, d), lambda i: (layer, 0, 0), pipeline_mode=pl.Buffered(1)),
            pl.BlockSpec((None,) + ws_bf.shape[1:], lambda i: (layer, 0, 0, 0)),
            const(bs3.shape),
            const((1, D_GMLP)),
            const((1, d)),
        ],
        out_specs=pl.BlockSpec((tm, d), lambda i: (i, 0)),
        out_shape=jax.ShapeDtypeStruct((s, d), F32),
        scratch_shapes=[pltpu.VMEM((tm, d), BF16), pltpu.VMEM((d, d), BF16)],
        compiler_params=pltpu.CompilerParams(
            dimension_semantics=("arbitrary",), vmem_limit_bytes=VMEM_LIMIT),
        name="outproj",
    )(ya, proj, yb, proj, proj, proj, x2, wout_bf, ws_bf, bs3, vgain, pgain)


def kernel(x, w_in, pre_gain, post_gain, w_fourier, lambda_qk, diff_out_gain, sg_v_gain,
           w_spatial, b_spatial, w_out, rel_bias):
    b, s, d = x.shape
    depth = w_in.shape[0]
    assert b == 1 and w_in.shape[2] == D_IN
    x2 = x.reshape(s, d)

    q_scale = (HALF_DIM ** -0.5) * LOG2E
    colscale = jnp.ones((1, D_IN), F32).at[:, OFF_Q:OFF_Q + D_DIFF].set(q_scale)
    bias = _bias_tiles(rel_bias, T_ATT)
    bias_bounds = jnp.stack([jnp.max(rel_bias, axis=0), jnp.min(rel_bias, axis=0)], axis=1) * LOG2E
    consts = _dft_consts(s // FFT_N2, FFT_N2, GROUP)

    w_in_bf = w_in.astype(BF16)
    w_fourier_bf, w_spatial_bf = w_fourier.astype(BF16), w_spatial.astype(BF16)

    for l in range(depth):
        lambda_init = 0.8 - 0.6 * math.exp(-0.3 * l)
        proj, a_in, qt4, vt4 = _inproj(x2, pre_gain[l].reshape(1, d), w_in_bf, l, colscale)
        ya = _fourier_branch(a_in, w_fourier_bf, l, consts)
        yb = _attention(bias_bounds, lambda_qk[l], qt4, proj, vt4, bias,
                        diff_out_gain[l].reshape(1, D_DIFF), lambda_init)
        x2 = _outproj(ya, yb, proj, x2, w_out, w_spatial_bf, l,
                      b_spatial[l].reshape(D_GMLP // GROUP, CHUNK, 1),
                      sg_v_gain[l].reshape(1, D_GMLP), post_gain[l].reshape(1, d))
    return x2.reshape(b, s, d)
```

```python
import functools
import math

import numpy as np
import jax
import jax.numpy as jnp
from jax import lax
from jax.experimental import pallas as pl
from jax.experimental.pallas import tpu as pltpu

F32 = jnp.float32
BF16 = jnp.bfloat16

EPS = 1e-6
LOG2E = math.log2(math.e)
NEG_INIT = -1e30
EXP2_SPAN = 120.0

D_FOURIER = 512
D_DIFF = 1024
D_GMLP = 512
GROUP = 128
HEAD_DIM = 128
HALF_DIM = 64
N_HEADS = 8
CHUNK = 128
N_BUCKETS = 32
MAX_DISTANCE = 128
OFF_A_IN, OFF_A_GATE, OFF_Q, OFF_K, OFF_V, OFF_B_GATE, OFF_U, OFF_VSG, OFF_C_GATE = (
    0, 512, 1024, 2048, 3072, 4096, 5120, 5632, 6144)
D_IN = 6656

MXU_COLS = 256
TM_IN = 512
TN_IN = 3328
TM_OUT = 512
T_ATT = 512
TQ_ATT = 2048
N_BIAS_TILES = 4
ZERO_BIAS_TILE = 3
FFT_N2 = 128
FFT_ROWS = 8
FFT_STEP_ROWS = 32
VMEM_LIMIT = 56 * 1024 * 1024
VMEM_LIMIT_INPROJ = 62 * 1024 * 1024
assert TM_IN % T_ATT == 0


def _silu(x):
    return x * (1.0 / (1.0 + jnp.exp(-x)))


def _inproj_kernel(x_ref, g_ref, w_ref, cs_ref, proj_ref, ain_ref, qt_ref, vt_ref, a_scr):
    j = pl.program_id(1)

    @pl.when(j == 0)
    def _():
        x = x_ref[...]
        ms = jnp.mean(x * x, axis=-1, keepdims=True)
        a_scr[...] = ((x * lax.rsqrt(ms + EPS)) * g_ref[...]).astype(BF16)

    acc = jnp.dot(a_scr[...], w_ref[...], preferred_element_type=F32)
    scaled = acc * cs_ref[...]
    proj_ref[...] = scaled.astype(BF16)

    @pl.when(j == 0)
    def _():
        ain_ref[...] = acc[:, :D_FOURIER]

    tn = proj_ref.shape[1]
    for jj in range(D_IN // tn):
        heads = [(dst, (c - off) // HEAD_DIM, c - jj * tn)
                 for dst, off in ((qt_ref, OFF_Q), (vt_ref, OFF_V))
                 for c in range(off, off + D_DIFF, HEAD_DIM)
                 if jj * tn <= c < (jj + 1) * tn]
        if heads:
            @pl.when(j == jj)
            def _(heads=heads):
                for dst, h, c0 in heads:
                    for r in range(dst.shape[1]):
                        rows = slice(r * T_ATT, (r + 1) * T_ATT)
                        dst[h, r] = scaled[rows, c0:c0 + HEAD_DIM].T.astype(BF16)


def _inproj(x2, gain, w_bf, layer, colscale):
    s, d = x2.shape
    n = w_bf.shape[2]
    grid = (s // TM_IN, n // TN_IN)
    head_tiles = pl.BlockSpec((N_HEADS, TM_IN // T_ATT, HEAD_DIM, T_ATT),
                              lambda i, j: (0, i, 0, 0))
    return pl.pallas_call(
        _inproj_kernel,
        grid=grid,
        in_specs=[
            pl.BlockSpec((TM_IN, d), lambda i, j: (i, 0)),
            pl.BlockSpec((1, d), lambda i, j: (0, 0)),
            pl.BlockSpec((None, d, TN_IN), lambda i, j: (layer, 0, j)),
            pl.BlockSpec((1, TN_IN), lambda i, j: (0, j)),
        ],
        out_specs=[
            pl.BlockSpec((TM_IN, TN_IN), lambda i, j: (i, j)),
            pl.BlockSpec((TM_IN, D_FOURIER), lambda i, j: (i, 0)),
            head_tiles,
            head_tiles,
        ],
        out_shape=[
            jax.ShapeDtypeStruct((s, n), BF16),
            jax.ShapeDtypeStruct((s, D_FOURIER), F32),
            jax.ShapeDtypeStruct((N_HEADS, s // T_ATT, HEAD_DIM, T_ATT), BF16),
            jax.ShapeDtypeStruct((N_HEADS, s // T_ATT, HEAD_DIM, T_ATT), BF16),
        ],
        scratch_shapes=[pltpu.VMEM((TM_IN, d), BF16)],
        compiler_params=pltpu.CompilerParams(
            dimension_semantics=("arbitrary", "arbitrary"),
            vmem_limit_bytes=VMEM_LIMIT_INPROJ),
        name="inproj",
    )(x2, gain, w_bf, colscale)


def _dft_consts(n1, n2, c):
    def cs(n):
        idx = np.arange(n)
        ang = 2.0 * np.pi * ((idx[:, None] * idx[None, :]) % n) / n
        return np.cos(ang), np.sin(ang)

    c1, s1 = cs(n1)
    c2, s2 = cs(n2)
    cc, sc = cs(c)
    f1 = np.kron(np.concatenate([c1, -s1], axis=0), np.eye(FFT_ROWS))
    f2 = np.block([[c2, s2], [s2, -c2]])
    n = n1 * n2
    k1 = np.arange(n1)[:, None]
    m2 = np.arange(n2)[None, :]
    ang = 2.0 * np.pi * ((k1 * m2) % n) / n
    tw_c = np.cos(ang)[:, :, None]
    tw_s = np.sin(ang)[:, :, None]
    ch = np.kron(np.eye(2), np.concatenate([cc, -sc], axis=0))
    f32 = lambda a: jnp.asarray(a.astype(np.float32))
    return f32(f1), f32(f2), f32(tw_c), f32(tw_s), f32(ch)


def _fft1_kernel(f1_ref, x_ref, y_ref):
    n1, rows, c = x_ref.shape
    f1 = f1_ref[...].astype(BF16)
    for r0 in range(0, rows, FFT_ROWS):
        blk = slice(r0, r0 + FFT_ROWS)
        x = x_ref[:, blk, :].reshape(n1 * FFT_ROWS, c).astype(BF16)
        y = jnp.dot(f1, x, preferred_element_type=F32)
        y_ref[:, blk, :] = y.reshape(y_ref.shape[0], FFT_ROWS, c)


def _fft1(f1, x3):
    n1, n2, c = x3.shape
    return pl.pallas_call(
        _fft1_kernel,
        grid=(n2 // FFT_STEP_ROWS,),
        in_specs=[pl.BlockSpec(f1.shape, lambda j: (0, 0)),
                  pl.BlockSpec((n1, FFT_STEP_ROWS, c), lambda j: (0, j, 0))],
        out_specs=pl.BlockSpec((2 * n1, FFT_STEP_ROWS, c), lambda j: (0, j, 0)),
        out_shape=jax.ShapeDtypeStruct((2 * n1, n2, c), F32),
        compiler_params=pltpu.CompilerParams(
            dimension_semantics=("arbitrary",), vmem_limit_bytes=VMEM_LIMIT),
        name="fft_stage1",
    )(f1, x3)


def _fft2_kernel(y_ref, twc_ref, tws_ref, f2_ref, ch_ref, wf_ref, o_ref, *, scale):
    f2 = f2_ref[...].astype(BF16)
    ch = ch_ref[...].astype(BF16)
    _, rows, n2, c = y_ref.shape
    ps, qs = [], []
    for r in range(rows):
        yr = y_ref[0, r]
        yi = y_ref[1, r]
        tc = twc_ref[r]
        ts = tws_ref[r]
        zr = yr * tc + yi * ts
        zi = yi * tc - yr * ts
        z = jnp.concatenate([zr, zi], axis=0).astype(BF16)
        pq = jnp.dot(f2, z, preferred_element_type=F32)
        ps.append(pq[:n2].astype(BF16))
        qs.append(pq[n2:].astype(BF16))
    p = jnp.concatenate(ps, axis=0)
    q = jnp.concatenate(qs, axis=0)
    fs = []
    for g in range(0, c // GROUP, 2):
        c0, c1 = slice(g * GROUP, (g + 1) * GROUP), slice((g + 1) * GROUP, (g + 2) * GROUP)
        pq_g = jnp.concatenate([p[:, c0], q[:, c0], p[:, c1], q[:, c1]], axis=1)
        fs.append(jnp.dot(pq_g, ch, preferred_element_type=F32))
    f = (jnp.concatenate(fs, axis=1) * scale).astype(BF16)
    ya = jnp.dot(f, wf_ref[...], preferred_element_type=F32)
    for r in range(rows):
        o_ref[:, r, :] = ya[r * n2:(r + 1) * n2]


def _fft2(y4, twc, tws, f2, ch, wf_bf, layer, scale):
    _, n1, n2, c = y4.shape
    const = lambda shape: pl.BlockSpec(shape, lambda k: tuple(0 for _ in shape))
    return pl.pallas_call(
        functools.partial(_fft2_kernel, scale=scale),
        grid=(n1 // FFT_ROWS,),
        in_specs=[
            pl.BlockSpec((2, FFT_ROWS, n2, c), lambda k: (0, k, 0, 0)),
            pl.BlockSpec((FFT_ROWS, n2, 1), lambda k: (k, 0, 0)),
            pl.BlockSpec((FFT_ROWS, n2, 1), lambda k: (k, 0, 0)),
            const((2 * n2, 2 * n2)),
            const(ch.shape),
            pl.BlockSpec((None, c, c), lambda k: (layer, 0, 0)),
        ],
        out_specs=pl.BlockSpec((n2, FFT_ROWS, c), lambda k: (0, k, 0)),
        out_shape=jax.ShapeDtypeStruct((n2, n1, c), F32),
        compiler_params=pltpu.CompilerParams(
            dimension_semantics=("arbitrary",), vmem_limit_bytes=VMEM_LIMIT),
        name="fft_stage2",
    )(y4, twc, tws, f2, ch, wf_bf)


def _fourier_branch(a_in, wf_bf, layer, consts):
    s, c = a_in.shape
    n2 = FFT_N2
    n1 = s // n2
    f1, f2, twc, tws, ch = consts
    y = _fft1(f1, a_in.reshape(n1, n2, c))
    scale = 1.0 / math.sqrt(s * GROUP)
    out = _fft2(y.reshape(2, n1, n2, c), twc, tws, f2, ch, wf_bf, layer, scale)
    return out.reshape(s, c)


def _t5_bucket(rel):
    nb = N_BUCKETS // 2
    ret = (rel > 0).astype(jnp.int32) * nb
    n = jnp.abs(rel)
    max_exact = nb // 2
    nf = jnp.maximum(n, 1).astype(jnp.float32)
    large = max_exact + (jnp.log(nf / max_exact) / math.log(MAX_DISTANCE / max_exact)
                         * (nb - max_exact)).astype(jnp.int32)
    large = jnp.minimum(large, nb - 1)
    return ret + jnp.where(n < max_exact, n, large)


def _bias_kernel(rb_ref, bk_ref, o_ref):
    h = pl.program_id(0)
    half = N_BUCKETS // 2
    t = bk_ref.shape[1]
    blk = MAX_DISTANCE
    for tile, d in enumerate((-1, 0, 1)):
        for kb in range(t // blk):
            for qb in range(t // blk):
                rel_lo = d * t + kb * blk - (qb * blk + blk - 1)
                rel_hi = d * t + kb * blk + blk - 1 - qb * blk
                rows, cols = slice(kb * blk, (kb + 1) * blk), slice(qb * blk, (qb + 1) * blk)
                if rel_hi <= -blk:
                    val = jnp.full((blk, blk), rb_ref[half - 1, h], F32)
                elif rel_lo >= blk:
                    val = jnp.full((blk, blk), rb_ref[N_BUCKETS - 1, h], F32)
                else:
                    bk = bk_ref[tile, rows, cols]
                    val = jnp.zeros(bk.shape, F32)
                    for b in range(0 if rel_lo <= 0 else half, half if rel_hi < 0 else N_BUCKETS):
                        val = jnp.where(bk == b, rb_ref[b, h], val)
                o_ref[0, tile, rows, cols] = val * LOG2E
    o_ref[0, ZERO_BIAS_TILE] = jnp.zeros((t, t), F32)


def _bias_tiles(rel_bias, t):
    kk = jnp.arange(t, dtype=jnp.int32)[:, None]
    rr = jnp.arange(t, dtype=jnp.int32)[None, :]
    rel = jnp.stack([d * t + kk - rr for d in (-1, 0, 1)])
    bucket = _t5_bucket(rel)
    return pl.pallas_call(
        _bias_kernel,
        grid=(N_HEADS,),
        in_specs=[pl.BlockSpec(memory_space=pltpu.SMEM),
                  pl.BlockSpec(bucket.shape, lambda h: (0, 0, 0))],
        out_specs=pl.BlockSpec((1, N_BIAS_TILES, t, t), lambda h: (h, 0, 0, 0)),
        out_shape=jax.ShapeDtypeStruct((N_HEADS, N_BIAS_TILES, t, t), F32),
        compiler_params=pltpu.CompilerParams(
            dimension_semantics=("arbitrary",), vmem_limit_bytes=VMEM_LIMIT),
        name="bias_tiles",
    )(rel_bias, bucket)


def _attn_kernel(bnd_ref, lq_ref, qt_ref, k_ref, vt_ref, bias_ref, gate_ref, gain_ref, o_ref,
                 q2_scr, m_scr, l_scr, acc_scr, kabs_scr, mreg_scr, pa_scr, pb_scr,
                 *, lambda_init, n_tiles):
    tq = qt_ref.shape[1] * qt_ref.shape[3]
    t = vt_ref.shape[-1]
    nqb = tq // t
    h = pl.program_id(0)
    i = pl.program_id(1)

    @pl.when(i == 0)
    def _():
        ka = jnp.max(jnp.abs(k_ref[...].astype(F32)), axis=0, keepdims=True)
        kabs_scr[...] = jnp.broadcast_to(ka, (8, HEAD_DIM)).T

    qt = jnp.concatenate([qt_ref[0, r] for r in range(qt_ref.shape[1])], axis=1)
    row = lax.broadcasted_iota(jnp.int32, qt.shape, 0)
    zero = jnp.zeros_like(qt)
    q2_scr[:, :tq] = jnp.where(row < HALF_DIM, qt, zero)
    q2_scr[:, tq:] = jnp.where(row >= HALF_DIM, qt, zero)
    l_scr[...] = jnp.zeros(l_scr.shape, F32)
    acc_scr[...] = jnp.zeros(acc_scr.shape, F32)

    c_left = bias_ref[0, 0, 0:1, t - 1:t]
    c_right = bias_ref[0, 2, t - 1:t, 0:1]

    wq = jnp.abs(qt.astype(F32)) * kabs_scr[:, 0:1]
    hb = jnp.concatenate([jnp.sum(wq[:HALF_DIM], axis=0, keepdims=True),
                          jnp.sum(wq[HALF_DIM:], axis=0, keepdims=True)], axis=1)
    b_max = bnd_ref[h, 0]
    b_min = bnd_ref[h, 1]
    fixed_ok = 2.0 * jnp.max(hb) + (b_max - b_min) <= EXP2_SPAN

    def block_relation(j, col0):
        e = j - (nqb * i + (col0 % tq) // t)
        tile_idx = jnp.where(jnp.abs(e) <= 1, e + 1, ZERO_BIAS_TILE)
        region = jnp.where(e < -1, 0, jnp.where(e > 1, 2, 1))
        return tile_idx, region

    def step(j_p, p_out, j_v, p_in, with_bias):
        if j_p is not None:
            kt = k_ref[pl.ds(pl.multiple_of(j_p * t, t), t), :]
        if j_v is not None:
            vt = vt_ref[0, j_v]
        for c0 in range(0, 2 * tq, MXU_COLS):
            cs = slice(c0, c0 + MXU_COLS)
            if j_p is not None:
                tile_idx, region = block_relation(j_p, c0)
                s = jnp.dot(kt, q2_scr[:, cs], preferred_element_type=F32)
                if with_bias:
                    b0 = c0 % t
                    s = s + bias_ref[0, tile_idx, :, b0:b0 + MXU_COLS]
                p = jnp.exp2(s - mreg_scr[region, :, cs])
                l_scr[:, cs] += jnp.sum(p, axis=0, keepdims=True)
                p_out[:, cs] = p.astype(BF16)
            if j_v is not None:
                acc_scr[:, cs] += jnp.dot(vt, p_in[:, cs], preferred_element_type=F32)

    def near_band(j):
        return jnp.logical_and(j >= nqb * i - 1, j <= nqb * i + nqb)

    def step_pair(j, last):
        def run(with_bias):
            step(j + 1, pb_scr, j, pa_scr, with_bias)
            step(None if last else j + 2, pa_scr, j + 1, pb_scr, with_bias)

        near = near_band(j + 1) if last else jnp.logical_or(near_band(j + 1), near_band(j + 2))
        lax.cond(near, lambda: run(True), lambda: run(False))

    @pl.when(fixed_ok)
    def _():
        m0 = hb + b_max
        mreg_scr[0] = m0 - c_left
        mreg_scr[1] = m0
        mreg_scr[2] = m0 - c_right

        def pair_body(jj, carry):
            step_pair(2 * jj, last=False)
            return carry

        step(0, pa_scr, None, None, True)
        lax.fori_loop(0, n_tiles // 2 - 1, pair_body, 0)
        step_pair(n_tiles - 2, last=True)

    def online_step(j):
        kt = k_ref[pl.ds(pl.multiple_of(j * t, t), t), :]
        vt = vt_ref[0, j]
        for c0 in range(0, 2 * tq, t):
            cs = slice(c0, c0 + t)
            tile_idx, region = block_relation(j, c0)
            c_far = jnp.where(region == 0, c_left, jnp.where(region == 2, c_right, 0.0))
            s = (jnp.dot(kt, q2_scr[:, cs], preferred_element_type=F32)
                 + (bias_ref[0, tile_idx] + c_far))
            m_old = m_scr[:, cs]
            m_new = jnp.maximum(m_old, jnp.max(s, axis=0, keepdims=True))
            p = jnp.exp2(s - m_new)
            alpha = jnp.exp2(m_old - m_new)
            l_scr[:, cs] = alpha * l_scr[:, cs] + jnp.sum(p, axis=0, keepdims=True)
            pv = jnp.dot(vt, p.astype(BF16), preferred_element_type=F32)
            acc_scr[:, cs] = alpha * acc_scr[:, cs] + pv
            m_scr[:, cs] = m_new

    @pl.when(jnp.logical_not(fixed_ok))
    def _():
        def body(j, carry):
            online_step(j)
            return carry

        m_scr[...] = jnp.full(m_scr.shape, NEG_INIT, F32)
        lax.fori_loop(0, n_tiles, body, 0)

    lq = lq_ref[...]
    lam = (jnp.exp(jnp.sum(lq[0:1] * lq[1:2], axis=-1, keepdims=True))
           - jnp.exp(jnp.sum(lq[2:3] * lq[3:4], axis=-1, keepdims=True)) + lambda_init)
    accn = acc_scr[...] * (1.0 / l_scr[...])
    o_t = accn[:, :tq] - lam * accn[:, tq:]
    ms = jnp.mean(o_t * o_t, axis=0, keepdims=True)
    o = (o_t * lax.rsqrt(ms + EPS)).T
    y = (o * gain_ref[...]) * (1.0 - lambda_init)
    o_ref[...] = (y * _silu(gate_ref[...].astype(F32))).astype(BF16)


def _attention(bnd, lq, qt4, proj, vt4, bias, gain, lambda_init):
    s = proj.shape[0]
    tq = TQ_ATT
    t = vt4.shape[-1]
    assert t == bias.shape[-1] and tq % t == 0
    n_tiles = s // t
    kblk = OFF_K // HEAD_DIM
    gblk = OFF_B_GATE // HEAD_DIM
    kern = functools.partial(_attn_kernel, lambda_init=lambda_init, n_tiles=n_tiles)
    return pl.pallas_call(
        kern,
        grid=(N_HEADS, s // tq),
        in_specs=[
            pl.BlockSpec(memory_space=pltpu.SMEM),
            pl.BlockSpec((4, HALF_DIM), lambda h, i: (0, 0)),
            pl.BlockSpec((1, tq // t, HEAD_DIM, t), lambda h, i: (h, i, 0, 0)),
            pl.BlockSpec((s, HEAD_DIM), lambda h, i: (0, kblk + h)),
            pl.BlockSpec((1, n_tiles, HEAD_DIM, t), lambda h, i: (h, 0, 0, 0)),
            pl.BlockSpec((1, N_BIAS_TILES, t, t), lambda h, i: (h, 0, 0, 0)),
            pl.BlockSpec((tq, HEAD_DIM), lambda h, i: (i, gblk + h)),
            pl.BlockSpec((1, HEAD_DIM), lambda h, i: (0, h)),
        ],
        out_specs=pl.BlockSpec((tq, HEAD_DIM), lambda h, i: (i, h)),
        out_shape=jax.ShapeDtypeStruct((s, D_DIFF), BF16),
        scratch_shapes=[
            pltpu.VMEM((HEAD_DIM, 2 * tq), BF16),
            pltpu.VMEM((1, 2 * tq), F32),
            pltpu.VMEM((1, 2 * tq), F32),
            pltpu.VMEM((HEAD_DIM, 2 * tq), F32),
            pltpu.VMEM((HEAD_DIM, 8), F32),
            pltpu.VMEM((3, 1, 2 * tq), F32),
            pltpu.VMEM((t, 2 * tq), BF16),
            pltpu.VMEM((t, 2 * tq), BF16),
        ],
        compiler_params=pltpu.CompilerParams(
            dimension_semantics=("arbitrary", "arbitrary"), vmem_limit_bytes=VMEM_LIMIT),
        name="diff_attention",
    )(bnd, lq, qt4, proj, vt4, bias, proj, gain)


def _outproj_kernel(ya_ref, ag_ref, yb_ref, u_ref, vsg_ref, cg_ref, x_ref, wout_ref, ws_ref,
                    bs_ref, vgain_ref, pgain_ref, o_ref, y_scr, wbf_scr):
    tm = x_ref.shape[0]

    @pl.when(pl.program_id(0) == 0)
    def _():
        wbf_scr[...] = wout_ref[...].astype(BF16)

    y_scr[:, 0:D_FOURIER] = (ya_ref[...] * _silu(ag_ref[...].astype(F32))).astype(BF16)
    y_scr[:, D_FOURIER:D_FOURIER + D_DIFF] = yb_ref[...]
    for g in range(D_GMLP // GROUP):
        cols = slice(g * GROUP, (g + 1) * GROUP)
        v = vsg_ref[:, cols].astype(F32)
        ms = jnp.mean(v * v, axis=-1, keepdims=True)
        vn = ((v * lax.rsqrt(ms + EPS)) * vgain_ref[:, cols]).astype(BF16)
        w = ws_ref[g]
        b = bs_ref[g]
        for c in range(tm // CHUNK):
            rows = slice(c * CHUNK, (c + 1) * CHUNK)
            mixed = jnp.dot(w, vn[rows], preferred_element_type=F32) + b
            gate = cg_ref[rows, cols].astype(F32)
            yc = (u_ref[rows, cols].astype(F32) * mixed) * _silu(gate)
            y_scr[rows, D_FOURIER + D_DIFF + g * GROUP:D_FOURIER + D_DIFF + (g + 1) * GROUP] = (
                yc.astype(BF16))
    y = jnp.dot(y_scr[...], wbf_scr[...], preferred_element_type=F32)
    ms = jnp.mean(y * y, axis=-1, keepdims=True)
    o_ref[...] = x_ref[...] + (y * lax.rsqrt(ms + EPS)) * pgain_ref[...]


def _outproj(ya, yb, proj, x2, wout_bf, ws_bf, layer, bs3, vgain, pgain):
    s, d = x2.shape
    tm = TM_OUT
    ablk = OFF_A_GATE // D_FOURIER
    ublk, vblk, cblk = OFF_U // D_GMLP, OFF_VSG // D_GMLP, OFF_C_GATE // D_GMLP
    const = lambda shape: pl.BlockSpec(shape, lambda i: tuple(0 for _ in shape))
    return pl.pallas_call(
        _outproj_kernel,
        grid=(s // tm,),
        in_specs=[
            pl.BlockSpec((tm, D_FOURIER), lambda i: (i, 0)),
            pl.BlockSpec((tm, D_FOURIER), lambda i: (i, ablk)),
            pl.BlockSpec((tm, D_DIFF), lambda i: (i, 0)),
            pl.BlockSpec((tm, D_GMLP), lambda i: (i, ublk)),
            pl.BlockSpec((tm, D_GMLP), lambda i: (i, vblk)),
            pl.BlockSpec((tm, D_GMLP), lambda i: (i, cblk)),
            pl.BlockSpec((tm, d), lambda i: (i, 0)),
            pl.BlockSpec((None, d, d), lambda i: (layer, 0, 0), pipeline_mode=pl.Buffered(1)),
            pl.BlockSpec((None,) + ws_bf.shape[1:], lambda i: (layer, 0, 0, 0)),
            const(bs3.shape),
            const((1, D_GMLP)),
            const((1, d)),
        ],
        out_specs=pl.BlockSpec((tm, d), lambda i: (i, 0)),
        out_shape=jax.ShapeDtypeStruct((s, d), F32),
        scratch_shapes=[pltpu.VMEM((tm, d), BF16), pltpu.VMEM((d, d), BF16)],
        compiler_params=pltpu.CompilerParams(
            dimension_semantics=("arbitrary",), vmem_limit_bytes=VMEM_LIMIT),
        name="outproj",
    )(ya, proj, yb, proj, proj, proj, x2, wout_bf, ws_bf, bs3, vgain, pgain)


def kernel(x, w_in, pre_gain, post_gain, w_fourier, lambda_qk, diff_out_gain, sg_v_gain,
           w_spatial, b_spatial, w_out, rel_bias):
    b, s, d = x.shape
    depth = w_in.shape[0]
    assert b == 1 and w_in.shape[2] == D_IN
    x2 = x.reshape(s, d)

    q_scale = (HALF_DIM ** -0.5) * LOG2E
    colscale = jnp.ones((1, D_IN), F32).at[:, OFF_Q:OFF_Q + D_DIFF].set(q_scale)
    bias = _bias_tiles(rel_bias, T_ATT)
    bias_bounds = jnp.stack([jnp.max(rel_bias, axis=0), jnp.min(rel_bias, axis=0)], axis=1) * LOG2E
    consts = _dft_consts(s // FFT_N2, FFT_N2, GROUP)

    w_in_bf = w_in.astype(BF16)
    w_fourier_bf, w_spatial_bf = w_fourier.astype(BF16), w_spatial.astype(BF16)

    for l in range(depth):
        lambda_init = 0.8 - 0.6 * math.exp(-0.3 * l)
        proj, a_in, qt4, vt4 = _inproj(x2, pre_gain[l].reshape(1, d), w_in_bf, l, colscale)
        ya = _fourier_branch(a_in, w_fourier_bf, l, consts)
        yb = _attention(bias_bounds, lambda_qk[l], qt4, proj, vt4, bias,
                        diff_out_gain[l].reshape(1, D_DIFF), lambda_init)
        x2 = _outproj(ya, yb, proj, x2, w_out, w_spatial_bf, l,
                      b_spatial[l].reshape(D_GMLP // GROUP, CHUNK, 1),
                      sg_v_gain[l].reshape(1, D_GMLP), post_gain[l].reshape(1, d))
    return x2.reshape(b, s, d)
```
